```python
import jax, jax.numpy as jnp
from jax import lax
import numpy as np

D_MODEL = 1024
BATCH = 8
SEQ = 4096
DEPTH = 2

N_EVEN = (DEPTH + 1) // 2
N_ODD = DEPTH // 2
EPS = 1e-6
GRID_W = 64

MEM_LEN = 256
MEM_HEADS = 4
MEM_HEAD_DIM = 64
MEM_WIDTH = MEM_HEADS * MEM_HEAD_DIM

NA_HEADS = 12
NA_HEAD_DIM = 64
NA_WIDTH = NA_HEADS * NA_HEAD_DIM
NA_WIN_ROWS = 8
NA_WIN_COLS = 16
NA_QBLOCK = 16
NA_BAND = 32

GLA_HEADS = 4
GLA_DK = 96
GLA_DV = 192
GLA_KW = GLA_HEADS * GLA_DK
GLA_VW = GLA_HEADS * GLA_DV
GLA_GATE_RANK = 16
GLA_GATE_NORM = 16.0
GLA_CHUNK = 64

D_FF = 2816
N_EXPERTS = 8
TOP_K = 2
D_FF_EXPERT = 3584
MOE_BLOCK = 256

MIX_WIDTH = NA_WIDTH + MEM_WIDTH
EVEN_IN = 3 * NA_WIDTH + MEM_WIDTH
ODD_IN = 2 * GLA_KW + 2 * GLA_VW + 2 * GLA_GATE_RANK + MEM_WIDTH

kernel_name = "hybrid_natten_gla_moe_encoder"


def rms_norm(x, g):
    xf = x.astype(jnp.float32)
    y = xf * lax.rsqrt(jnp.mean(xf * xf, axis=-1, keepdims=True) + EPS)
    return (y * g.astype(jnp.float32)).astype(x.dtype)


def swiglu(x, w_gate, w_up, w_down):
    return (jax.nn.silu(x @ w_gate) * (x @ w_up)) @ w_down


def memory_attention(q, mem_k, mem_v):
    s = jnp.einsum('bthd,bmhd->bhtm', q, mem_k).astype(jnp.float32) * (MEM_HEAD_DIM ** -0.5)
    p = jax.nn.softmax(s, axis=-1).astype(mem_v.dtype)
    o = jnp.einsum('bhtm,bmhd->bthd', p, mem_v)
    return o.reshape(q.shape[0], q.shape[1], MEM_WIDTH)


def neighbourhood_attention(q, k, v, rpb):
    B, T, H, dh = q.shape
    rows = T // GRID_W
    kr = min(NA_WIN_ROWS, rows)
    n_cb = GRID_W // NA_QBLOCK
    qc = np.arange(GRID_W).reshape(n_cb, NA_QBLOCK)
    band0 = np.clip(np.arange(n_cb) * NA_QBLOCK - NA_WIN_COLS // 2, 0, GRID_W - NA_BAND)
    kc = band0[:, None] + np.arange(NA_BAND)[None, :]
    cs = np.clip(qc - NA_WIN_COLS // 2, 0, GRID_W - NA_WIN_COLS)
    col_ok = (kc[:, None, :] >= cs[:, :, None]) & (kc[:, None, :] < cs[:, :, None] + NA_WIN_COLS)
    dc = np.clip(kc[:, None, :] - qc[:, :, None], -(NA_WIN_COLS - 1), NA_WIN_COLS - 1) + NA_WIN_COLS - 1
    col_mask = jnp.asarray(col_ok)
    bias_c = rpb[:, :, dc].astype(jnp.float32)
    kg = k.reshape(B, rows, GRID_W, H, dh)
    vg = v.reshape(B, rows, GRID_W, H, dh)
    qg = jnp.moveaxis(q.reshape(B, rows, n_cb, NA_QBLOCK, H, dh), 1, 0)
    scale = dh ** -0.5

    def row_step(args):
        q_r, r = args
        r0 = jnp.clip(r - kr // 2, 0, rows - kr)
        k_band = lax.dynamic_slice_in_dim(kg, r0, kr, axis=1)[:, :, kc]
        v_band = lax.dynamic_slice_in_dim(vg, r0, kr, axis=1)[:, :, kc]
        dr = r0 + jnp.arange(kr) - r + NA_WIN_ROWS - 1
        bias = jnp.transpose(bias_c[:, dr], (0, 2, 3, 1, 4))
        s = jnp.einsum('bjqhd,bajuhd->bhjqau', q_r, k_band).astype(jnp.float32) * scale + bias[None]
        s = jnp.where(col_mask[None, None, :, :, None, :], s, -jnp.inf)
        shp = s.shape
        p = jax.nn.softmax(s.reshape(shp[0], shp[1], shp[2], shp[3], kr * NA_BAND), axis=-1)
        p = p.reshape(shp).astype(v_band.dtype)
        return jnp.einsum('bhjqau,bajuhd->bjqhd', p, v_band)

    o = lax.map(row_step, (qg, jnp.arange(rows)))
    return jnp.moveaxis(o, 0, 1).reshape(B, T, H * dh)


def gla_direction(q, k, v, g, include_diag):
    C = q.shape[3]
    b = jnp.cumsum(g, axis=3)
    b_last = b[:, :, :, -1:, :]
    q_t = q * jnp.exp(b)
    k_t = k * jnp.exp(-b)
    mask = jnp.asarray(np.tril(np.ones((C, C), dtype=bool), 0 if include_diag else -1))
    a = jnp.where(mask, jnp.einsum('bhncd,bhnsd->bhncs', q_t, k_t), 0.0)
    o_intra = jnp.einsum('bhncs,bhnsv->bhncv', a, v)
    ds = jnp.einsum('bhncd,bhncv->bhndv', k * jnp.exp(b_last - b), v)
    decay = jnp.exp(b_last[:, :, :, 0, :])

    def step(s, inp):
        d_n, ds_n = inp
        return d_n[..., None] * s + ds_n, s

    s0 = jnp.zeros(ds.shape[:2] + ds.shape[3:], jnp.float32)
    _, s_in = lax.scan(step, s0, (jnp.moveaxis(decay, 2, 0), jnp.moveaxis(ds, 2, 0)))
    s_in = jnp.moveaxis(s_in, 0, 2)
    return o_intra + jnp.einsum('bhncd,bhndv->bhncv', q_t, s_in)


def gla_mixer(q, k, v, gk_f, gk_b, gate, g_norm):
    B, T, H, _ = q.shape
    n = T // GLA_CHUNK

    def to_chunks(a):
        return jnp.transpose(a.reshape(B, n, GLA_CHUNK, H, a.shape[-1]), (0, 3, 1, 2, 4)).astype(jnp.float32)

    def from_chunks(a):
        return jnp.transpose(a, (0, 2, 3, 1, 4)).reshape(B, T, H, a.shape[-1])

    def flip(a):
        return a[:, ::-1]

    qs = q * (GLA_DK ** -0.5)
    fwd = gla_direction(to_chunks(qs), to_chunks(k), to_chunks(v), to_chunks(gk_f), True)
    bwd = gla_direction(to_chunks(flip(qs)), to_chunks(flip(k)), to_chunks(flip(v)), to_chunks(flip(gk_b)), False)
    o = from_chunks(fwd) + flip(from_chunks(bwd))
    o = o * lax.rsqrt(jnp.mean(o * o, axis=-1, keepdims=True) + EPS) * g_norm.astype(jnp.float32)
    o = o.reshape(B, T, GLA_VW) * jax.nn.silu(gate.astype(jnp.float32))
    return o.astype(q.dtype)


def moe_swiglu(x, w_router, w_gate, w_up, w_down):
    B, T, D = x.shape
    xf = x.reshape(-1, D)
    n = xf.shape[0]
    logits = (xf @ w_router).astype(jnp.float32)
    top_val, top_idx = lax.top_k(logits, TOP_K)
    weights = jax.nn.softmax(top_val, axis=-1)
    e_flat = top_idx.reshape(-1)
    tok_flat = jnp.repeat(jnp.arange(n), TOP_K)
    w_flat = weights.reshape(-1)
    order = jnp.argsort(e_flat)
    e_s, tok_s, w_s = e_flat[order], tok_flat[order], w_flat[order]
    counts = jnp.bincount(e_flat, length=N_EXPERTS)
    start = jnp.cumsum(counts) - counts
    padded = (counts + MOE_BLOCK - 1) // MOE_BLOCK * MOE_BLOCK
    pend = jnp.cumsum(padded)
    pstart = pend - padded
    dest = pstart[e_s] + jnp.arange(n * TOP_K) - start[e_s]
    cap = n * TOP_K + N_EXPERTS * MOE_BLOCK
    n_blocks = cap // MOE_BLOCK
    x_pad = jnp.zeros((cap, D), x.dtype).at[dest].set(xf[tok_s])
    block_expert = jnp.clip(jnp.searchsorted(pend, jnp.arange(n_blocks) * MOE_BLOCK, side='right'), 0, N_EXPERTS - 1)

    def expert_block(args):
        xb, e = args
        return swiglu(xb, w_gate[e], w_up[e], w_down[e])

    y_pad = lax.map(expert_block, (x_pad.reshape(n_blocks, MOE_BLOCK, D), block_expert)).reshape(cap, D)
    y = y_pad[dest] * w_s[:, None].astype(x.dtype)
    return jax.ops.segment_sum(y, tok_s, num_segments=n).reshape(B, T, D)


def setup_inputs(seed: int = 0) -> dict:
    key = jax.random.key(seed)
    ks = iter(jax.random.split(key, 32))
    D = D_MODEL

    def nrm(shape, scale):
        return scale * jax.random.normal(next(ks), shape, jnp.float32)

    def gain(shape):
        return 1.0 + nrm(shape, 0.02)

    return {
        "x": nrm((BATCH, SEQ, D), 1.0),
        "mem": nrm((BATCH, MEM_LEN, D), 1.0),
        "even_norm1": gain((N_EVEN, D)),
        "even_w_in": nrm((N_EVEN, D, EVEN_IN), D ** -0.5),
        "even_rpb": nrm((N_EVEN, NA_HEADS, 2 * NA_WIN_ROWS - 1, 2 * NA_WIN_COLS - 1), 0.2),
        "even_w_out": nrm((N_EVEN, MIX_WIDTH, D), MIX_WIDTH ** -0.5),
        "even_norm2": gain((N_EVEN, D)),
        "even_w_gate": nrm((N_EVEN, D, D_FF), D ** -0.5),
        "even_w_up": nrm((N_EVEN, D, D_FF), D ** -0.5),
        "even_w_down": nrm((N_EVEN, D_FF, D), D_FF ** -0.5),
        "odd_norm1": gain((N_ODD, D)),
        "odd_w_in": nrm((N_ODD, D, ODD_IN), D ** -0.5),
        "odd_w_gk_fwd": nrm((N_ODD, GLA_GATE_RANK, GLA_KW), GLA_GATE_RANK ** -0.5),
        "odd_b_gk_fwd": nrm((N_ODD, GLA_KW), 0.1),
        "odd_w_gk_bwd": nrm((N_ODD, GLA_GATE_RANK, GLA_KW), GLA_GATE_RANK ** -0.5),
        "odd_b_gk_bwd": nrm((N_ODD, GLA_KW), 0.1),
        "odd_g_norm": gain((N_ODD, GLA_DV)),
        "odd_w_out": nrm((N_ODD, MIX_WIDTH, D), MIX_WIDTH ** -0.5),
        "odd_norm2": gain((N_ODD, D)),
        "odd_w_router": nrm((N_ODD, D, N_EXPERTS), D ** -0.5),
        "odd_w_gate": nrm((N_ODD, N_EXPERTS, D, D_FF_EXPERT), D ** -0.5),
        "odd_w_up": nrm((N_ODD, N_EXPERTS, D, D_FF_EXPERT), D ** -0.5),
        "odd_w_down": nrm((N_ODD, N_EXPERTS, D_FF_EXPERT, D), D_FF_EXPERT ** -0.5),
        "mem_norm": gain((D,)),
        "w_mem_kv": nrm((D, 2 * MEM_WIDTH), D ** -0.5),
        "final_norm": gain((D,)),
    }


def reference(x, mem, even_norm1, even_w_in, even_rpb, even_w_out, even_norm2, even_w_gate, even_w_up,
              even_w_down, odd_norm1, odd_w_in, odd_w_gk_fwd, odd_b_gk_fwd, odd_w_gk_bwd, odd_b_gk_bwd,
              odd_g_norm, odd_w_out, odd_norm2, odd_w_router, odd_w_gate, odd_w_up, odd_w_down,
              mem_norm, w_mem_kv, final_norm):
    B, T, _ = x.shape
    M = mem.shape[1]
    mem_kv = (rms_norm(mem, mem_norm) @ w_mem_kv).reshape(B, M, 2, MEM_HEADS, MEM_HEAD_DIM)
    mem_k, mem_v = mem_kv[:, :, 0], mem_kv[:, :, 1]
    h = x
    for i in range(DEPTH):
        j = i // 2
        if i % 2 == 0:
            u = rms_norm(h, even_norm1[j])
            q, k, v, qm = jnp.split(u @ even_w_in[j], [NA_WIDTH, 2 * NA_WIDTH, 3 * NA_WIDTH], axis=-1)
            mix = neighbourhood_attention(q.reshape(B, T, NA_HEADS, NA_HEAD_DIM),
                                          k.reshape(B, T, NA_HEADS, NA_HEAD_DIM),
                                          v.reshape(B, T, NA_HEADS, NA_HEAD_DIM), even_rpb[j])
            mo = memory_attention(qm.reshape(B, T, MEM_HEADS, MEM_HEAD_DIM), mem_k, mem_v)
            h = h + jnp.concatenate([mix, mo], axis=-1) @ even_w_out[j]
            h = h + swiglu(rms_norm(h, even_norm2[j]), even_w_gate[j], even_w_up[j], even_w_down[j])
        else:
            u = rms_norm(h, odd_norm1[j])
            splits = [GLA_KW, 2 * GLA_KW, 2 * GLA_KW + GLA_VW, 2 * GLA_KW + 2 * GLA_VW,
                      2 * GLA_KW + 2 * GLA_VW + GLA_GATE_RANK, 2 * GLA_KW + 2 * GLA_VW + 2 * GLA_GATE_RANK]
            q, k, v, g, zf, zb, qm = jnp.split(u @ odd_w_in[j], splits, axis=-1)
            gk_f = jax.nn.log_sigmoid((zf @ odd_w_gk_fwd[j] + odd_b_gk_fwd[j]).astype(jnp.float32)) / GLA_GATE_NORM
            gk_b = jax.nn.log_sigmoid((zb @ odd_w_gk_bwd[j] + odd_b_gk_bwd[j]).astype(jnp.float32)) / GLA_GATE_NORM
            mix = gla_mixer(q.reshape(B, T, GLA_HEADS, GLA_DK), k.reshape(B, T, GLA_HEADS, GLA_DK),
                            v.reshape(B, T, GLA_HEADS, GLA_DV),
                            gk_f.reshape(B, T, GLA_HEADS, GLA_DK), gk_b.reshape(B, T, GLA_HEADS, GLA_DK),
                            g, odd_g_norm[j])
            mo = memory_attention(qm.reshape(B, T, MEM_HEADS, MEM_HEAD_DIM), mem_k, mem_v)
            h = h + jnp.concatenate([mix, mo], axis=-1) @ odd_w_out[j]
            h = h + moe_swiglu(rms_norm(h, odd_norm2[j]), odd_w_router[j], odd_w_gate[j], odd_w_up[j], odd_w_down[j])
    return rms_norm(h, final_norm)
```

```python
import functools

import numpy as np
import jax
import jax.numpy as jnp
from jax import lax
from jax.experimental import pallas as pl
from jax.experimental.pallas import tpu as pltpu

F32 = jnp.float32
BF16 = jnp.bfloat16
I32 = jnp.int32

EPS = 1e-6
GRID_W = 64
HEAD_DIM = 64
ATTN_SCALE = HEAD_DIM ** -0.5
NA_HEADS = 12
NA_WIDTH = NA_HEADS * HEAD_DIM
NA_WIN_ROWS = 8
NA_WIN_COLS = 16
MEM_HEADS = 4
MEM_WIDTH = MEM_HEADS * HEAD_DIM
GLA_HEADS = 4
GLA_DK = 96
GLA_DV = 192
GLA_DK_PAD = 128
GLA_DV_PAD = 256
GLA_RANK = 16
GLA_GATE_NORM = 16.0
GLA_CHUNK = 64
N_EXPERTS = 8
TOP_K = 2
LANES = 128
V7X_VMEM_BYTES = 64 * 1024 * 1024

ROW_TILE = 512
MOE_ROWS = 512
MOE_FF_TILE = 896
FFN_FF_TILE = 1408
PROJ_COL_CHUNK = 512


def _compiler_params(semantics, vmem_mib):
    assert vmem_mib * 1024 * 1024 < V7X_VMEM_BYTES
    return pltpu.CompilerParams(dimension_semantics=semantics, vmem_limit_bytes=vmem_mib * 1024 * 1024)


def _rms(x, gain):
    return x * lax.rsqrt(jnp.mean(x * x, axis=-1, keepdims=True) + EPS) * gain


def _dot(a, b):
    return jnp.dot(a, b, preferred_element_type=F32)


def _dot_nt(a, b):
    return lax.dot_general(a, b, (((1,), (1,)), ((), ())), preferred_element_type=F32)


def _dot_tn(a, b):
    return lax.dot_general(a, b, (((0,), (0,)), ((), ())), preferred_element_type=F32)


def _silu(x):
    return x * jax.nn.sigmoid(x)


def _norm_matmul_kernel(x_ref, gain_ref, w_ref, o_ref):
    y = _rms(x_ref[...], gain_ref[...]).astype(BF16)
    for j in range(0, o_ref.shape[-1], PROJ_COL_CHUNK):
        o_ref[:, j:j + PROJ_COL_CHUNK] = _dot(y, w_ref[:, j:j + PROJ_COL_CHUNK]).astype(o_ref.dtype)


def _norm_matmul(x2d, gain, w):
    n, d = x2d.shape
    nout = w.shape[1]
    assert n % ROW_TILE == 0 and nout % PROJ_COL_CHUNK == 0
    return pl.pallas_call(
        _norm_matmul_kernel,
        out_shape=jax.ShapeDtypeStruct((n, nout), BF16),
        grid=(n // ROW_TILE,),
        in_specs=[
            pl.BlockSpec((ROW_TILE, d), lambda i: (i, 0)),
            pl.BlockSpec((1, d), lambda i: (0, 0)),
            pl.BlockSpec((d, nout), lambda i: (0, 0), pipeline_mode=pl.Buffered(1)),
        ],
        out_specs=pl.BlockSpec((ROW_TILE, nout), lambda i: (i, 0)),
        compiler_params=_compiler_params(("arbitrary",), 40),
        name="norm_matmul",
    )(x2d, gain.reshape(1, d), w)


def _softmax_pv(s, v_pair):
    m = jnp.max(s, axis=-1, keepdims=True)
    e = jnp.exp(s - m)
    l = jnp.sum(e, axis=-1, keepdims=True)
    return _dot(e.astype(BF16), v_pair) / l


def _memory_attention_pair(q_pair, k_pair, v_pair, low_half):
    zero = jnp.zeros_like(q_pair)
    outs = []
    for sel in (low_half, jnp.logical_not(low_half)):
        s = _dot_nt(jnp.where(sel, q_pair, zero), k_pair)
        outs.append(_softmax_pv(s, v_pair))
    return jnp.where(low_half, outs[0], outs[1])


def _na_mixer_kernel(q_ref, k_ref, v_ref, qm_ref, mk_ref, mv_ref, bias_ref, wo_ref, h_ref, o_ref, mix_ref,
                     *, rows_per_step, n_rows):
    g = pl.program_id(1)
    low_row = lax.broadcasted_iota(I32, (GRID_W, LANES), 1) < HEAD_DIM
    band = NA_WIN_ROWS * GRID_W

    def row_body(r, carry):
        row = g * rows_per_step + r
        r0 = jnp.clip(row - NA_WIN_ROWS // 2, 0, n_rows - NA_WIN_ROWS)
        shift = NA_WIN_ROWS - 1 - (row - r0)
        q_row = q_ref[0, pl.ds(pl.multiple_of(r * GRID_W, GRID_W), GRID_W), :] * ATTN_SCALE
        k_off = pl.multiple_of(r0 * GRID_W, GRID_W)
        for p in range(NA_HEADS // 2):
            lanes = slice(p * LANES, (p + 1) * LANES)
            q_pair = q_row[:, lanes]
            k_pair = k_ref[0, pl.ds(k_off, band), lanes]
            v_pair = v_ref[0, pl.ds(k_off, band), lanes]
            zero = jnp.zeros_like(q_pair)
            outs = []
            for hh, sel in enumerate((low_row, jnp.logical_not(low_row))):
                head = 2 * p + hh
                s = _dot_nt(jnp.where(sel, q_pair, zero), k_pair)
                bias = jnp.concatenate(
                    [bias_ref[head, shift + 2 * m] for m in range(NA_WIN_ROWS // 2)], axis=-1)
                outs.append(_softmax_pv(s + bias, v_pair))
            mix_ref[pl.ds(pl.multiple_of(r * GRID_W, GRID_W), GRID_W), lanes] = (
                jnp.where(low_row, outs[0], outs[1]).astype(BF16))
        return carry

    lax.fori_loop(0, rows_per_step, row_body, 0)

    tm = rows_per_step * GRID_W
    low = lax.broadcasted_iota(I32, (tm, LANES), 1) < HEAD_DIM
    qm = qm_ref[0] * ATTN_SCALE
    for p in range(MEM_HEADS // 2):
        lanes = slice(p * LANES, (p + 1) * LANES)
        o = _memory_attention_pair(qm[:, lanes], mk_ref[0, :, lanes], mv_ref[0, :, lanes], low)
        mix_ref[:, NA_WIDTH + p * LANES:NA_WIDTH + (p + 1) * LANES] = o.astype(BF16)
    o_ref[0] = h_ref[0] + _dot(mix_ref[...], wo_ref[...])


def _na_bias_table(rpb):
    qc = np.arange(GRID_W)[:, None]
    kc = np.arange(GRID_W)[None, :]
    cs = np.clip(qc - NA_WIN_COLS // 2, 0, GRID_W - NA_WIN_COLS)
    ok = (kc >= cs) & (kc < cs + NA_WIN_COLS)
    dc = np.clip(kc - qc, -(NA_WIN_COLS - 1), NA_WIN_COLS - 1) + NA_WIN_COLS - 1
    t = jnp.where(jnp.asarray(ok)[None, None], rpb[:, :, dc].astype(F32), -jnp.inf)
    return jnp.concatenate([t[:, :-1], t[:, 1:]], axis=-1)


def _na_mixer(proj, mem_kv, bias_tab, w_out, h, rows_per_step):
    b, t, d = h.shape
    n_rows = t // GRID_W
    assert n_rows >= NA_WIN_ROWS and n_rows % rows_per_step == 0
    tm = rows_per_step * GRID_W
    m = mem_kv.shape[1]
    kernel = functools.partial(_na_mixer_kernel, rows_per_step=rows_per_step, n_rows=n_rows)
    kv_blocks = NA_WIDTH // MEM_WIDTH
    return pl.pallas_call(
        kernel,
        out_shape=jax.ShapeDtypeStruct((b, t, d), F32),
        grid=(b, n_rows // rows_per_step),
        in_specs=[
            pl.BlockSpec((1, tm, NA_WIDTH), lambda i, j: (i, j, 0)),
            pl.BlockSpec((1, t, NA_WIDTH), lambda i, j: (i, 0, 1)),
            pl.BlockSpec((1, t, NA_WIDTH), lambda i, j: (i, 0, 2)),
            pl.BlockSpec((1, tm, MEM_WIDTH), lambda i, j: (i, j, 3 * kv_blocks)),
            pl.BlockSpec((1, m, MEM_WIDTH), lambda i, j: (i, 0, 0)),
            pl.BlockSpec((1, m, MEM_WIDTH), lambda i, j: (i, 0, 1)),
            pl.BlockSpec(bias_tab.shape, lambda i, j: (0, 0, 0, 0), pipeline_mode=pl.Buffered(1)),
            pl.BlockSpec(w_out.shape, lambda i, j: (0, 0), pipeline_mode=pl.Buffered(1)),
            pl.BlockSpec((1, tm, d), lambda i, j: (i, j, 0)),
        ],
        out_specs=pl.BlockSpec((1, tm, d), lambda i, j: (i, j, 0)),
        scratch_shapes=[pltpu.VMEM((tm, d), BF16)],
        compiler_params=_compiler_params(("arbitrary", "arbitrary"), 56),
        name="na_mixer",
    )(proj, proj, proj, proj, mem_kv, mem_kv, bias_tab, w_out, h)


def _ffn_kernel(x_ref, gain_ref, wg_ref, wu_ref, wd_ref, o_ref, xn_ref):
    @pl.when(pl.program_id(1) == 0)
    def _():
        x = x_ref[...]
        xn_ref[...] = _rms(x, gain_ref[...]).astype(BF16)
        o_ref[...] = x

    xn = xn_ref[...]
    act = (_silu(_dot(xn, wg_ref[...])) * _dot(xn, wu_ref[...])).astype(BF16)
    o_ref[...] += _dot(act, wd_ref[...])


def _ffn(x2d, gain, w_gate, w_up, w_down):
    n, d = x2d.shape
    ff = w_gate.shape[1]
    assert n % ROW_TILE == 0 and ff % FFN_FF_TILE == 0
    return pl.pallas_call(
        _ffn_kernel,
        out_shape=jax.ShapeDtypeStruct((n, d), F32),
        grid=(n // ROW_TILE, ff // FFN_FF_TILE),
        in_specs=[
            pl.BlockSpec((ROW_TILE, d), lambda i, f: (i, 0)),
            pl.BlockSpec((1, d), lambda i, f: (0, 0)),
            pl.BlockSpec((d, FFN_FF_TILE), lambda i, f: (0, f)),
            pl.BlockSpec((d, FFN_FF_TILE), lambda i, f: (0, f)),
            pl.BlockSpec((FFN_FF_TILE, d), lambda i, f: (f, 0)),
        ],
        out_specs=pl.BlockSpec((ROW_TILE, d), lambda i, f: (i, 0)),
        scratch_shapes=[pltpu.VMEM((ROW_TILE, d), BF16)],
        compiler_params=_compiler_params(("arbitrary", "arbitrary"), 48),
        name="ffn",
    )(x2d, gain.reshape(1, d), w_gate, w_up, w_down)


def _split3(x):
    x1 = x.astype(BF16)
    r1 = x - x1.astype(F32)
    x2 = r1.astype(BF16)
    x3 = (r1 - x2.astype(F32)).astype(BF16)
    return x1, x2, x3


def _gla_kernel(q_ref, k_ref, v_ref, gate_ref, z_ref, wz_ref, bz_ref, gn_ref, o_ref, ob_ref, st_ref, *, n_chunks):
    c = GLA_CHUNK
    row = lax.broadcasted_iota(I32, (c, c), 0)
    col = lax.broadcasted_iota(I32, (c, c), 1)
    lower = jnp.where(row >= col, 1.0, 0.0).astype(BF16)
    upper = jnp.where(row <= col, 1.0, 0.0).astype(BF16)
    q_scale = GLA_DK ** -0.5

    def log_decay(c0, half):
        lanes = slice(half * GLA_DK_PAD, (half + 1) * GLA_DK_PAD)
        z = _dot(z_ref[0, pl.ds(c0, c), :], wz_ref[0, :, lanes]) + bz_ref[0, :, lanes]
        return (jnp.minimum(z, 0.0) - jnp.log1p(jnp.exp(-jnp.abs(z)))) * (1.0 / GLA_GATE_NORM)

    def chunk(c0, half, tri, keep, last_row):
        q = q_ref[0, pl.ds(c0, c), :].astype(F32) * q_scale
        k = k_ref[0, pl.ds(c0, c), :].astype(F32)
        v = v_ref[0, pl.ds(c0, c), :]
        g1, g2, g3 = _split3(log_decay(c0, half))
        b = _dot(tri, g1) + _dot(tri, g2) + _dot(tri, g3)
        b_last = b[last_row:last_row + 1, :]
        q_t = (q * jnp.exp(b)).astype(BF16)
        k_t = (k * jnp.exp(-b)).astype(BF16)
        k_d = (k * jnp.exp(b_last - b)).astype(BF16)
        a = jnp.where(keep, _dot_nt(q_t, k_t), 0.0).astype(BF16)
        state = st_ref[...]
        o = _dot(a, v) + _dot_nt(q_t, state.astype(BF16))
        st_ref[...] = state * jnp.exp(b_last) + _dot_tn(v, k_d)
        return o

    st_ref[...] = jnp.zeros_like(st_ref)

    def bwd_body(i, carry):
        c0 = pl.multiple_of((n_chunks - 1 - i) * c, c)
        ob_ref[pl.ds(c0, c), :] = chunk(c0, 1, upper, col > row, 0)
        return carry

    lax.fori_loop(0, n_chunks, bwd_body, 0)
    st_ref[...] = jnp.zeros_like(st_ref)

    def fwd_body(i, carry):
        c0 = pl.multiple_of(i * c, c)
        o = chunk(c0, 0, lower, row >= col, c - 1) + ob_ref[pl.ds(c0, c), :]
        ms = jnp.sum(o * o, axis=-1, keepdims=True) * (1.0 / GLA_DV)
        y = o * lax.rsqrt(ms + EPS) * gn_ref[...]
        o_ref[0, pl.ds(c0, c), :] = (y * _silu(gate_ref[0, pl.ds(c0, c), :].astype(F32))).astype(BF16)
        return carry

    lax.fori_loop(0, n_chunks, fwd_body, 0)


def _gla(proj, wz, bz, g_norm_pad):
    b, t, _ = proj.shape
    assert t % GLA_CHUNK == 0
    kernel = functools.partial(_gla_kernel, n_chunks=t // GLA_CHUNK)
    k_blk = GLA_HEADS
    v_blk = 2 * GLA_HEADS * GLA_DK_PAD // GLA_DV_PAD
    g_blk = v_blk + GLA_HEADS
    z_blk = g_blk + GLA_HEADS + MEM_WIDTH // GLA_DV_PAD
    return pl.pallas_call(
        kernel,
        out_shape=jax.ShapeDtypeStruct((b, t, GLA_HEADS * GLA_DV_PAD), BF16),
        grid=(b, GLA_HEADS),
        in_specs=[
            pl.BlockSpec((1, t, GLA_DK_PAD), lambda i, h: (i, 0, h)),
            pl.BlockSpec((1, t, GLA_DK_PAD), lambda i, h: (i, 0, k_blk + h)),
            pl.BlockSpec((1, t, GLA_DV_PAD), lambda i, h: (i, 0, v_blk + h)),
            pl.BlockSpec((1, t, GLA_DV_PAD), lambda i, h: (i, 0, g_blk + h)),
            pl.BlockSpec((1, t, GLA_DV_PAD), lambda i, h: (i, 0, z_blk)),
            pl.BlockSpec((1, GLA_DV_PAD, 2 * GLA_DK_PAD), lambda i, h: (h, 0, 0)),
            pl.BlockSpec((1, 1, 2 * GLA_DK_PAD), lambda i, h: (h, 0, 0)),
            pl.BlockSpec((1, GLA_DV_PAD), lambda i, h: (0, 0)),
        ],
        out_specs=pl.BlockSpec((1, t, GLA_DV_PAD), lambda i, h: (i, 0, h)),
        scratch_shapes=[pltpu.VMEM((t, GLA_DV_PAD), F32), pltpu.VMEM((GLA_DV_PAD, GLA_DK_PAD), F32)],
        compiler_params=_compiler_params(("arbitrary", "arbitrary"), 40),
        name="gla",
    )(proj, proj, proj, proj, proj, wz, bz, g_norm_pad)


def _mix_out_router_kernel(mix_ref, qm_ref, mk_ref, mv_ref, wmix_ref, wmo_ref, h_ref, gain_ref, wr_hi_ref, wr_lo_ref,
                           h_out_ref, xn_ref, route_ref):
    tm = mix_ref.shape[1]
    low = lax.broadcasted_iota(I32, (tm, LANES), 1) < HEAD_DIM
    qm = qm_ref[0] * ATTN_SCALE
    h = h_ref[0] + _dot(mix_ref[0], wmix_ref[...])
    for p in range(MEM_HEADS // 2):
        lanes = slice(p * LANES, (p + 1) * LANES)
        o = _memory_attention_pair(qm[:, lanes], mk_ref[0, :, lanes], mv_ref[0, :, lanes], low)
        h = h + _dot(o.astype(BF16), wmo_ref[lanes, :])
    h_out_ref[0] = h
    xn = _rms(h, gain_ref[...])
    xn_ref[0] = xn

    x_hi = xn.astype(BF16)
    x_lo = (xn - x_hi.astype(F32)).astype(BF16)
    logits = _dot(x_hi, wr_hi_ref[...]) + (_dot(x_lo, wr_hi_ref[...]) + _dot(x_hi, wr_lo_ref[...]))
    lane = lax.broadcasted_iota(I32, (tm, LANES), 1)
    neg = jnp.float32(-jnp.inf)
    logits = jnp.where(lane < N_EXPERTS, logits, neg)
    m1 = jnp.max(logits, axis=-1, keepdims=True)
    i1 = jnp.min(jnp.where(logits == m1, lane, LANES), axis=-1, keepdims=True)
    rest = jnp.where(lane == i1, neg, logits)
    m2 = jnp.max(rest, axis=-1, keepdims=True)
    i2 = jnp.min(jnp.where(rest == m2, lane, LANES), axis=-1, keepdims=True)
    e2 = jnp.exp(m2 - m1)
    den = 1.0 + e2
    route = jnp.where(lane == 0, i1.astype(F32),
                      jnp.where(lane == 1, i2.astype(F32),
                                jnp.where(lane == 2, 1.0 / den, jnp.where(lane == 3, e2 / den, 0.0))))
    route_ref[0] = route


def _mix_out_router(mix, proj, mem_kv, w_mix, w_mo, h, gain, wr_hi, wr_lo):
    b, t, d = h.shape
    tm = ROW_TILE
    m = mem_kv.shape[1]
    qm_blk = (2 * GLA_HEADS * GLA_DK_PAD + 2 * GLA_HEADS * GLA_DV_PAD) // MEM_WIDTH
    const = lambda shape: pl.BlockSpec(shape, lambda i, j: (0,) * len(shape), pipeline_mode=pl.Buffered(1))
    tile = lambda w: pl.BlockSpec((1, tm, w), lambda i, j: (i, j, 0))
    return pl.pallas_call(
        _mix_out_router_kernel,
        out_shape=(jax.ShapeDtypeStruct((b, t, d), F32), jax.ShapeDtypeStruct((b, t, d), F32),
                   jax.ShapeDtypeStruct((b, t, LANES), F32)),
        grid=(b, t // tm),
        in_specs=[
            tile(mix.shape[-1]),
            pl.BlockSpec((1, tm, MEM_WIDTH), lambda i, j: (i, j, qm_blk)),
            pl.BlockSpec((1, m, MEM_WIDTH), lambda i, j: (i, 0, 0)),
            pl.BlockSpec((1, m, MEM_WIDTH), lambda i, j: (i, 0, 1)),
            const(w_mix.shape), const(w_mo.shape),
            tile(d),
            const((1, d)), const(wr_hi.shape), const(wr_lo.shape),
        ],
        out_specs=(tile(d), tile(d), tile(LANES)),
        compiler_params=_compiler_params(("arbitrary", "arbitrary"), 40),
        name="mix_out_router",
    )(mix, proj, mem_kv, mem_kv, w_mix, w_mo, h, gain.reshape(1, d), wr_hi, wr_lo)


def _moe_kernel(be_ref, valid_ref, tok_ref, dst_ref, w_ref, x_hbm, wg_ref, wu_ref, wd_ref, y_hbm,
                rows_ref, xb_ref, acc_ref, sem):
    i = pl.program_id(0)
    f = pl.program_id(1)
    tm = rows_ref.shape[0]
    n_valid = valid_ref[i]

    def gather_row(j):
        return pltpu.make_async_copy(x_hbm.at[pl.ds(tok_ref[0, 0, j], 1)], rows_ref.at[pl.ds(j, 1)], sem.at[0])

    def scatter_row(j):
        return pltpu.make_async_copy(rows_ref.at[pl.ds(j, 1)], y_hbm.at[pl.ds(dst_ref[0, 0, j], 1)], sem.at[1])

    @pl.when(n_valid > 0)
    def _():
        @pl.when(f == 0)
        def _():
            def start(j, carry):
                gather_row(j).start()
                return carry

            def wait(j, carry):
                gather_row(j).wait()
                return carry

            lax.fori_loop(0, tm, start, 0)
            lax.fori_loop(0, tm, wait, 0)
            xb_ref[...] = rows_ref[...].astype(BF16)
            acc_ref[...] = jnp.zeros_like(acc_ref)

        xb = xb_ref[...]
        act = (_silu(_dot(xb, wg_ref[0])) * _dot(xb, wu_ref[0])).astype(BF16)
        acc_ref[...] += _dot(act, wd_ref[0])

        @pl.when(f == pl.num_programs(1) - 1)
        def _():
            rows_ref[...] = acc_ref[...] * w_ref[...]

            def start(j, carry):
                scatter_row(j).start()
                return carry

            def wait(j, carry):
                scatter_row(j).wait()
                return carry

            lax.fori_loop(0, n_valid, start, 0)
            lax.fori_loop(0, n_valid, wait, 0)


def _moe(xn2d, block_expert, block_valid, tok_sorted, dst_sorted, w_sorted, w_gate, w_up, w_down):
    n, d = xn2d.shape
    tm = MOE_ROWS
    nblk = tok_sorted.shape[0] // tm
    ff = w_gate.shape[-1]
    assert ff % MOE_FF_TILE == 0
    nf = ff // MOE_FF_TILE

    def f_idx(i, f, valid):
        return jnp.where(valid[i] > 0, f, nf - 1)

    grid_spec = pltpu.PrefetchScalarGridSpec(
        num_scalar_prefetch=2,
        grid=(nblk, nf),
        in_specs=[
            pl.BlockSpec((1, 1, tm), lambda i, f, be, valid: (i, 0, 0), memory_space=pltpu.SMEM),
            pl.BlockSpec((1, 1, tm), lambda i, f, be, valid: (i, 0, 0), memory_space=pltpu.SMEM),
            pl.BlockSpec((tm, 1), lambda i, f, be, valid: (i, 0)),
            pl.BlockSpec(memory_space=pl.ANY),
            pl.BlockSpec((1, d, MOE_FF_TILE), lambda i, f, be, valid: (be[i], 0, f_idx(i, f, valid))),
            pl.BlockSpec((1, d, MOE_FF_TILE), lambda i, f, be, valid: (be[i], 0, f_idx(i, f, valid))),
            pl.BlockSpec((1, MOE_FF_TILE, d), lambda i, f, be, valid: (be[i], f_idx(i, f, valid), 0)),
        ],
        out_specs=pl.BlockSpec(memory_space=pl.ANY),
        scratch_shapes=[pltpu.VMEM((tm, d), F32), pltpu.VMEM((tm, d), BF16), pltpu.VMEM((tm, d), F32),
                        pltpu.SemaphoreType.DMA((2,))],
    )
    return pl.pallas_call(
        _moe_kernel,
        grid_spec=grid_spec,
        out_shape=jax.ShapeDtypeStruct((TOP_K * n, d), F32),
        compiler_params=_compiler_params(("arbitrary", "arbitrary"), 48),
        name="moe",
    )(block_expert, block_valid, tok_sorted.reshape(nblk, 1, tm), dst_sorted.reshape(nblk, 1, tm),
      w_sorted.reshape(-1, 1), xn2d, w_gate, w_up, w_down)


def _route_plan(route2d):
    n = route2d.shape[0]
    tm = MOE_ROWS
    n_slots = n * TOP_K
    cap = n_slots + N_EXPERTS * tm
    e_flat = route2d[:, :TOP_K].astype(I32).reshape(-1)
    w_flat = route2d[:, TOP_K:2 * TOP_K].reshape(-1)
    onehot = (e_flat[:, None] == jnp.arange(N_EXPERTS, dtype=I32)[None, :]).astype(I32)
    csum = jnp.cumsum(onehot, axis=0)
    rank = jnp.sum(csum * onehot, axis=1) - 1
    counts = csum[-1]
    padded = (counts + tm - 1) // tm * tm
    pend = jnp.cumsum(padded)
    pstart = pend - padded
    dest = jnp.sum(pstart[None, :] * onehot, axis=1) + rank
    slot = jnp.arange(n_slots, dtype=I32)
    tok_sorted = jnp.zeros((cap,), I32).at[dest].set(slot // TOP_K, unique_indices=True)
    dst_sorted = jnp.zeros((cap,), I32).at[dest].set((slot % TOP_K) * n + slot // TOP_K, unique_indices=True)
    w_sorted = jnp.zeros((cap,), F32).at[dest].set(w_flat, unique_indices=True)
    block_start = jnp.arange(cap // tm, dtype=I32) * tm
    block_expert = jnp.clip(jnp.searchsorted(pend, block_start, side='right'), 0, N_EXPERTS - 1).astype(I32)
    real_end = (pstart + counts)[block_expert]
    block_valid = jnp.where(block_start < pend[-1], jnp.clip(real_end - block_start, 0, tm), 0).astype(I32)
    return tok_sorted, dst_sorted, w_sorted, block_expert, block_valid


def _final_kernel(h_ref, y0_ref, y1_ref, gain_ref, o_ref):
    o_ref[...] = _rms(h_ref[...] + (y0_ref[...] + y1_ref[...]), gain_ref[...])


def _final(h2d, y, gain):
    n, d = h2d.shape
    tm = ROW_TILE
    return pl.pallas_call(
        _final_kernel,
        out_shape=jax.ShapeDtypeStruct((n, d), F32),
        grid=(n // tm,),
        in_specs=[
            pl.BlockSpec((tm, d), lambda i: (i, 0)),
            pl.BlockSpec((tm, d), lambda i: (i, 0)),
            pl.BlockSpec((tm, d), lambda i: (i + n // tm, 0)),
            pl.BlockSpec((1, d), lambda i: (0, 0)),
        ],
        out_specs=pl.BlockSpec((tm, d), lambda i: (i, 0)),
        compiler_params=_compiler_params(("arbitrary",), 32),
        name="final_norm",
    )(h2d, y, y, gain.reshape(1, d))


def _pad_heads(w, heads, width, padded):
    lead = w.shape[:-1]
    w = w.reshape(lead + (heads, width))
    w = jnp.pad(w, [(0, 0)] * len(lead) + [(0, 0), (0, padded - width)])
    return w.reshape(lead + (heads * padded,))


def _odd_layouts(w_in, w_gk_fwd, b_gk_fwd, w_gk_bwd, b_gk_bwd, g_norm, w_out):
    kw = GLA_HEADS * GLA_DK
    vw = GLA_HEADS * GLA_DV
    q, k, v, g, z, qm = jnp.split(w_in, [kw, 2 * kw, 2 * kw + vw, 2 * kw + 2 * vw, 2 * kw + 2 * vw + 2 * GLA_RANK], axis=-1)
    z = jnp.pad(z, ((0, 0), (0, GLA_DV_PAD - 2 * GLA_RANK)))
    w_in_pad = jnp.concatenate([
        _pad_heads(q, GLA_HEADS, GLA_DK, GLA_DK_PAD), _pad_heads(k, GLA_HEADS, GLA_DK, GLA_DK_PAD),
        _pad_heads(v, GLA_HEADS, GLA_DV, GLA_DV_PAD), _pad_heads(g, GLA_HEADS, GLA_DV, GLA_DV_PAD), qm, z], axis=-1)

    def per_head(a):
        return jnp.moveaxis(_pad_heads(a, GLA_HEADS, GLA_DK, GLA_DK_PAD).reshape(a.shape[0], GLA_HEADS, GLA_DK_PAD), 1, 0)

    wz = jnp.zeros((GLA_HEADS, GLA_DV_PAD, 2 * GLA_DK_PAD), F32)
    wz = wz.at[:, :GLA_RANK, :GLA_DK_PAD].set(per_head(w_gk_fwd))
    wz = wz.at[:, GLA_RANK:2 * GLA_RANK, GLA_DK_PAD:].set(per_head(w_gk_bwd))
    bz = jnp.concatenate([per_head(b_gk_fwd[None]), per_head(b_gk_bwd[None])], axis=-1)
    gn = jnp.pad(g_norm, (0, GLA_DV_PAD - GLA_DV)).reshape(1, GLA_DV_PAD)
    w_mix = _pad_heads(w_out[:vw].T, GLA_HEADS, GLA_DV, GLA_DV_PAD).T
    return w_in_pad.astype(BF16), wz.astype(BF16), bz.astype(F32), gn.astype(F32), w_mix.astype(BF16), w_out[vw:].astype(BF16)


def _router_split(w_router):
    w = jnp.pad(w_router.astype(F32), ((0, 0), (0, LANES - N_EXPERTS)))
    hi = w.astype(BF16)
    return hi, (w - hi.astype(F32)).astype(BF16)


def kernel(x, mem, even_norm1, even_w_in, even_rpb, even_w_out, even_norm2, even_w_gate, even_w_up, even_w_down, odd_norm1, odd_w_in, odd_w_gk_fwd, odd_b_gk_fwd, odd_w_gk_bwd, odd_b_gk_bwd, odd_g_norm, odd_w_out, odd_norm2, odd_w_router, odd_w_gate, odd_w_up, odd_w_down, mem_norm, w_mem_kv, final_norm):
    b, t, d = x.shape
    m = mem.shape[1]
    n = b * t
    depth = even_norm1.shape[0] + odd_norm1.shape[0]

    mem_kv = _norm_matmul(mem.reshape(b * m, d), mem_norm, w_mem_kv.astype(BF16)).reshape(b, m, 2 * MEM_WIDTH)
    h = x
    for layer in range(depth):
        j = layer // 2
        if layer % 2 == 0:
            proj = _norm_matmul(h.reshape(n, d), even_norm1[j], even_w_in[j].astype(BF16)).reshape(b, t, -1)
            h = _na_mixer(proj, mem_kv, _na_bias_table(even_rpb[j]), even_w_out[j].astype(BF16), h, rows_per_step=8)
            h = _ffn(h.reshape(n, d), even_norm2[j], even_w_gate[j].astype(BF16), even_w_up[j].astype(BF16),
                     even_w_down[j].astype(BF16)).reshape(b, t, d)
        else:
            w_in, wz, bz, gn, w_mix, w_mo = _odd_layouts(odd_w_in[j], odd_w_gk_fwd[j], odd_b_gk_fwd[j], odd_w_gk_bwd[j],
                                                        odd_b_gk_bwd[j], odd_g_norm[j], odd_w_out[j])
            proj = _norm_matmul(h.reshape(n, d), odd_norm1[j], w_in).reshape(b, t, -1)
            mix = _gla(proj, wz, bz, gn)
            wr_hi, wr_lo = _router_split(odd_w_router[j])
            h, xn, route = _mix_out_router(mix, proj, mem_kv, w_mix, w_mo, h, odd_norm2[j], wr_hi, wr_lo)
            tok_sorted, dst_sorted, w_sorted, block_expert, block_valid = _route_plan(route.reshape(n, LANES))
            y = _moe(xn.reshape(n, d), block_expert, block_valid, tok_sorted, dst_sorted, w_sorted,
                     odd_w_gate[j].astype(BF16), odd_w_up[j].astype(BF16), odd_w_down[j].astype(BF16))
            if layer == depth - 1:
                return _final(h.reshape(n, d), y, final_norm).reshape(b, t, d)
            h = h + (y[:n] + y[n:2 * n]).reshape(b, t, d)
    return _final(h.reshape(n, d), jnp.zeros((2 * n, d), F32), final_norm).reshape(b, t, d)
```

```python
import functools

import numpy as np
import jax
import jax.numpy as jnp
from jax import lax
from jax.experimental import pallas as pl
from jax.experimental.pallas import tpu as pltpu

F32 = jnp.float32
BF16 = jnp.bfloat16
I32 = jnp.int32

EPS = 1e-6
GRID_W = 64
HEAD_DIM = 64
ATTN_SCALE = HEAD_DIM ** -0.5
NA_HEADS = 12
NA_WIDTH = NA_HEADS * HEAD_DIM
NA_WIN_ROWS = 8
NA_WIN_COLS = 16
MEM_HEADS = 4
MEM_WIDTH = MEM_HEADS * HEAD_DIM
GLA_HEADS = 4
GLA_DK = 96
GLA_DV = 192
GLA_DK_PAD = 128
GLA_DV_PAD = 256
GLA_RANK = 16
GLA_GATE_NORM = 16.0
GLA_CHUNK = 64
GLA_BLOCK_CHUNKS = 4
GLA_HEADS_PER_STEP = 2
N_EXPERTS = 8
TOP_K = 2
LANES = 128
SUBLANES = 8
V7X_VMEM_BYTES = 64 * 1024 * 1024

ROW_TILE = 512
MOE_ROWS = 512
MOE_FF_TILE = 1792
PROJ_COL_CHUNK = 512


def _compiler_params(semantics, vmem_mib):
    assert vmem_mib * 1024 * 1024 < V7X_VMEM_BYTES
    return pltpu.CompilerParams(dimension_semantics=semantics, vmem_limit_bytes=vmem_mib * 1024 * 1024)


def _rms(x, gain):
    return x * lax.rsqrt(jnp.mean(x * x, axis=-1, keepdims=True) + EPS) * gain


def _dot(a, b):
    return jnp.dot(a, b, preferred_element_type=F32)


def _dot_nt(a, b):
    return lax.dot_general(a, b, (((1,), (1,)), ((), ())), preferred_element_type=F32)


def _dot_tn(a, b):
    return lax.dot_general(a, b, (((0,), (0,)), ((), ())), preferred_element_type=F32)


def _silu(x):
    return x * jax.nn.sigmoid(x)


def _norm_matmul_kernel(x_ref, gain_ref, w_ref, o_ref):
    y = _rms(x_ref[...], gain_ref[...]).astype(BF16)
    for j in range(0, o_ref.shape[-1], PROJ_COL_CHUNK):
        o_ref[:, j:j + PROJ_COL_CHUNK] = _dot(y, w_ref[:, j:j + PROJ_COL_CHUNK]).astype(o_ref.dtype)


def _norm_matmul(x2d, gain, w):
    n, d = x2d.shape
    nout = w.shape[1]
    assert n % ROW_TILE == 0 and nout % PROJ_COL_CHUNK == 0
    return pl.pallas_call(
        _norm_matmul_kernel,
        out_shape=jax.ShapeDtypeStruct((n, nout), BF16),
        grid=(n // ROW_TILE,),
        in_specs=[
            pl.BlockSpec((ROW_TILE, d), lambda i: (i, 0)),
            pl.BlockSpec((1, d), lambda i: (0, 0)),
            pl.BlockSpec((d, nout), lambda i: (0, 0), pipeline_mode=pl.Buffered(1)),
        ],
        out_specs=pl.BlockSpec((ROW_TILE, nout), lambda i: (i, 0)),
        compiler_params=_compiler_params(("arbitrary",), 40),
        name="norm_matmul",
    )(x2d, gain.reshape(1, d), w)


def _softmax_pv(s, v_pair):
    m = jnp.max(s, axis=-1, keepdims=True)
    e = jnp.exp(s - m)
    l = jnp.sum(e, axis=-1, keepdims=True)
    return _dot(e.astype(BF16), v_pair) / l


def _memory_attention_pair(q_pair, k_pair, v_pair, low_half):
    zero = jnp.zeros_like(q_pair)
    outs = []
    for sel in (low_half, jnp.logical_not(low_half)):
        s = _dot_nt(jnp.where(sel, q_pair, zero), k_pair)
        outs.append(_softmax_pv(s, v_pair))
    return jnp.where(low_half, outs[0], outs[1])


def _na_mixer_kernel(q_ref, k_ref, v_ref, qm_ref, mk_ref, mv_ref, bias_ref, wo_ref, h_ref, o_ref, mix_ref,
                     *, rows_per_step, n_rows):
    g = pl.program_id(1)
    low_row = lax.broadcasted_iota(I32, (GRID_W, LANES), 1) < HEAD_DIM
    band = NA_WIN_ROWS * GRID_W

    def row_body(r, carry):
        row = g * rows_per_step + r
        r0 = jnp.clip(row - NA_WIN_ROWS // 2, 0, n_rows - NA_WIN_ROWS)
        shift = NA_WIN_ROWS - 1 - (row - r0)
        q_row = q_ref[0, pl.ds(pl.multiple_of(r * GRID_W, GRID_W), GRID_W), :] * ATTN_SCALE
        k_off = pl.multiple_of(r0 * GRID_W, GRID_W)
        for p in range(NA_HEADS // 2):
            lanes = slice(p * LANES, (p + 1) * LANES)
            q_pair = q_row[:, lanes]
            k_pair = k_ref[0, pl.ds(k_off, band), lanes]
            v_pair = v_ref[0, pl.ds(k_off, band), lanes]
            zero = jnp.zeros_like(q_pair)
            outs = []
            for hh, sel in enumerate((low_row, jnp.logical_not(low_row))):
                head = 2 * p + hh
                s = _dot_nt(jnp.where(sel, q_pair, zero), k_pair)
                bias = jnp.concatenate(
                    [bias_ref[head, shift + 2 * m] for m in range(NA_WIN_ROWS // 2)], axis=-1)
                outs.append(_softmax_pv(s + bias, v_pair))
            mix_ref[pl.ds(pl.multiple_of(r * GRID_W, GRID_W), GRID_W), lanes] = (
                jnp.where(low_row, outs[0], outs[1]).astype(BF16))
        return carry

    lax.fori_loop(0, rows_per_step, row_body, 0)

    tm = rows_per_step * GRID_W
    low = lax.broadcasted_iota(I32, (tm, LANES), 1) < HEAD_DIM
    qm = qm_ref[0] * ATTN_SCALE
    for p in range(MEM_HEADS // 2):
        lanes = slice(p * LANES, (p + 1) * LANES)
        o = _memory_attention_pair(qm[:, lanes], mk_ref[0, :, lanes], mv_ref[0, :, lanes], low)
        mix_ref[:, NA_WIDTH + p * LANES:NA_WIDTH + (p + 1) * LANES] = o.astype(BF16)
    o_ref[0] = h_ref[0] + _dot(mix_ref[...], wo_ref[...])


def _na_bias_table(rpb):
    qc = np.arange(GRID_W)[:, None]
    kc = np.arange(GRID_W)[None, :]
    cs = np.clip(qc - NA_WIN_COLS // 2, 0, GRID_W - NA_WIN_COLS)
    ok = (kc >= cs) & (kc < cs + NA_WIN_COLS)
    dc = np.clip(kc - qc, -(NA_WIN_COLS - 1), NA_WIN_COLS - 1) + NA_WIN_COLS - 1
    t = jnp.where(jnp.asarray(ok)[None, None], rpb[:, :, dc].astype(F32), -jnp.inf)
    return jnp.concatenate([t[:, :-1], t[:, 1:]], axis=-1)


def _na_mixer(proj, mem_kv, bias_tab, w_out, h, rows_per_step):
    b, t, d = h.shape
    n_rows = t // GRID_W
    assert n_rows >= NA_WIN_ROWS and n_rows % rows_per_step == 0
    tm = rows_per_step * GRID_W
    m = mem_kv.shape[1]
    kernel = functools.partial(_na_mixer_kernel, rows_per_step=rows_per_step, n_rows=n_rows)
    kv_blocks = NA_WIDTH // MEM_WIDTH
    return pl.pallas_call(
        kernel,
        out_shape=jax.ShapeDtypeStruct((b, t, d), F32),
        grid=(b, n_rows // rows_per_step),
        in_specs=[
            pl.BlockSpec((1, tm, NA_WIDTH), lambda i, j: (i, j, 0)),
            pl.BlockSpec((1, t, NA_WIDTH), lambda i, j: (i, 0, 1)),
            pl.BlockSpec((1, t, NA_WIDTH), lambda i, j: (i, 0, 2)),
            pl.BlockSpec((1, tm, MEM_WIDTH), lambda i, j: (i, j, 3 * kv_blocks)),
            pl.BlockSpec((1, m, MEM_WIDTH), lambda i, j: (i, 0, 0)),
            pl.BlockSpec((1, m, MEM_WIDTH), lambda i, j: (i, 0, 1)),
            pl.BlockSpec(bias_tab.shape, lambda i, j: (0, 0, 0, 0), pipeline_mode=pl.Buffered(1)),
            pl.BlockSpec(w_out.shape, lambda i, j: (0, 0), pipeline_mode=pl.Buffered(1)),
            pl.BlockSpec((1, tm, d), lambda i, j: (i, j, 0)),
        ],
        out_specs=pl.BlockSpec((1, tm, d), lambda i, j: (i, j, 0)),
        scratch_shapes=[pltpu.VMEM((tm, d), BF16)],
        compiler_params=_compiler_params(("arbitrary", "arbitrary"), 56),
        name="na_mixer",
    )(proj, proj, proj, proj, mem_kv, mem_kv, bias_tab, w_out, h)


def _ffn_kernel(x_ref, gain_ref, wg_ref, wu_ref, wd_ref, o_ref):
    x = x_ref[...]
    xn = _rms(x, gain_ref[...]).astype(BF16)
    act = (_silu(_dot(xn, wg_ref[...])) * _dot(xn, wu_ref[...])).astype(BF16)
    o_ref[...] = x + _dot(act, wd_ref[...])


def _ffn(x2d, gain, w_gate, w_up, w_down):
    n, d = x2d.shape
    ff = w_gate.shape[1]
    assert n % ROW_TILE == 0
    resident = lambda shape: pl.BlockSpec(shape, lambda i: (0, 0), pipeline_mode=pl.Buffered(1))
    return pl.pallas_call(
        _ffn_kernel,
        out_shape=jax.ShapeDtypeStruct((n, d), F32),
        grid=(n // ROW_TILE,),
        in_specs=[
            pl.BlockSpec((ROW_TILE, d), lambda i: (i, 0)),
            resident((1, d)), resident((d, ff)), resident((d, ff)), resident((ff, d)),
        ],
        out_specs=pl.BlockSpec((ROW_TILE, d), lambda i: (i, 0)),
        compiler_params=_compiler_params(("arbitrary",), 56),
        name="ffn",
    )(x2d, gain.reshape(1, d), w_gate, w_up, w_down)


def _split3(x):
    x1 = x.astype(BF16)
    r1 = x - x1.astype(F32)
    x2 = r1.astype(BF16)
    x3 = (r1 - x2.astype(F32)).astype(BF16)
    return x1, x2, x3


def _gla_kernel(q_ref, k_ref, v_ref, gate_ref, z_ref, wz_ref, bz_ref, gn_ref, o_ref, part_ref, st_ref, *, n_blocks):
    c = GLA_CHUNK
    r = GLA_BLOCK_CHUNKS * c
    row = lax.broadcasted_iota(I32, (r, r), 0)
    col = lax.broadcasted_iota(I32, (r, r), 1)
    same = jnp.where(lax.shift_right_logical(row, 6) == lax.shift_right_logical(col, 6), 1.0, 0.0)
    assert c == 1 << 6
    lower = jnp.where(row >= col, same, 0.0)
    upper_incl = jnp.where(row <= col, same, 0.0)
    upper = jnp.where(row < col, same, 0.0)
    sum_ops = (lower.astype(BF16), upper_incl.astype(BF16))
    keeps = (lower > 0.0, upper > 0.0)
    q_scale = GLA_DK ** -0.5

    chains = [(hh, d) for hh in range(GLA_HEADS_PER_STEP) for d in (0, 1)]

    def blocks(r0s):
        z, q, k, v = [], [], [], []
        for hh, d in chains:
            dk = slice(hh * GLA_DK_PAD, (hh + 1) * GLA_DK_PAD)
            dv = slice(hh * GLA_DV_PAD, (hh + 1) * GLA_DV_PAD)
            gl = slice(d * GLA_DK_PAD, (d + 1) * GLA_DK_PAD)
            z.append(_dot(z_ref[0, pl.ds(r0s[d], r), :], wz_ref[hh, :, gl]) + bz_ref[hh, :, gl])
            q.append(q_ref[0, pl.ds(r0s[d], r), dk].astype(F32) * q_scale)
            k.append(k_ref[0, pl.ds(r0s[d], r), dk].astype(F32))
            v.append(v_ref[0, pl.ds(r0s[d], r), dv])
        n = range(len(chains))
        gs = [_split3((jnp.minimum(z[i], 0.0) - jnp.log1p(jnp.exp(-jnp.abs(z[i])))) * (1.0 / GLA_GATE_NORM)) for i in n]
        b = [_dot(sum_ops[chains[i][1]], jnp.concatenate(gs[i], axis=-1)) for i in n]
        b = [x[:, :GLA_DK_PAD] + x[:, GLA_DK_PAD:2 * GLA_DK_PAD] + x[:, 2 * GLA_DK_PAD:] for x in b]
        ends = [[x[ci * c + (0 if d else c - 1):ci * c + (1 if d else c)] for ci in range(GLA_BLOCK_CHUNKS)]
                for x, (_, d) in zip(b, chains)]
        b_end = [jnp.concatenate([jnp.broadcast_to(e, (c, GLA_DK_PAD)) for e in es], axis=0) for es in ends]
        q_t = [(q[i] * jnp.exp(b[i])).astype(BF16) for i in n]
        k_t = [(k[i] * jnp.exp(-b[i])).astype(BF16) for i in n]
        a = [jnp.where(keeps[chains[i][1]], _dot_nt(q_t[i], k_t[i]), 0.0).astype(BF16) for i in n]
        k_d = [(k[i] * jnp.exp(b_end[i] - b[i])).astype(BF16) for i in n]
        decay = [[jnp.exp(e) for e in es] for es in ends]
        intra = [_dot(a[i], v[i]) for i in n]
        state = [st_ref[i] for i in n]
        outs = [[None] * GLA_BLOCK_CHUNKS for _ in n]
        for step in range(GLA_BLOCK_CHUNKS):
            for i, (_, d) in enumerate(chains):
                ci = GLA_BLOCK_CHUNKS - 1 - step if d else step
                rows = slice(ci * c, (ci + 1) * c)
                outs[i][ci] = intra[i][rows] + _dot_nt(q_t[i][rows], state[i].astype(BF16))
                state[i] = state[i] * decay[i][ci] + _dot_tn(v[i][rows], k_d[i][rows])
        for i in n:
            st_ref[i] = state[i]
        return [jnp.concatenate(o, axis=0) for o in outs]

    def park(r0s, outs):
        for (hh, d), o in zip(chains, outs):
            part_ref[pl.ds(r0s[d], r), hh * GLA_DV_PAD:(hh + 1) * GLA_DV_PAD] = o

    def finish(r0s, outs):
        for (hh, d), o in zip(chains, outs):
            dv = slice(hh * GLA_DV_PAD, (hh + 1) * GLA_DV_PAD)
            o = o + part_ref[pl.ds(r0s[d], r), dv]
            ms = jnp.sum(o * o, axis=-1, keepdims=True) * (1.0 / GLA_DV)
            y = o * lax.rsqrt(ms + EPS) * gn_ref[...]
            o_ref[0, pl.ds(r0s[d], r), dv] = (y * _silu(gate_ref[0, pl.ds(r0s[d], r), dv].astype(F32))).astype(BF16)

    st_ref[...] = jnp.zeros_like(st_ref)

    def trip(epilogue, j, carry):
        r0s = (pl.multiple_of(j * r, r), pl.multiple_of((n_blocks - 1 - j) * r, r))
        epilogue(r0s, blocks(r0s))
        return carry

    lax.fori_loop(0, n_blocks // 2, functools.partial(trip, park), 0)
    lax.fori_loop(n_blocks // 2, n_blocks, functools.partial(trip, finish), 0)


def _gla(proj, wz, bz, g_norm_pad):
    b, t, _ = proj.shape
    rows = GLA_BLOCK_CHUNKS * GLA_CHUNK
    assert t % (2 * rows) == 0
    kernel = functools.partial(_gla_kernel, n_blocks=t // rows)
    hs = GLA_HEADS_PER_STEP
    groups = GLA_HEADS // hs
    kw, vw = hs * GLA_DK_PAD, hs * GLA_DV_PAD
    v_blk = 2 * groups * kw // vw
    z_col = 2 * GLA_HEADS * (GLA_DK_PAD + GLA_DV_PAD) + MEM_WIDTH
    return pl.pallas_call(
        kernel,
        out_shape=jax.ShapeDtypeStruct((b, t, GLA_HEADS * GLA_DV_PAD), BF16),
        grid=(b, groups),
        in_specs=[
            pl.BlockSpec((1, t, kw), lambda i, h: (i, 0, h)),
            pl.BlockSpec((1, t, kw), lambda i, h: (i, 0, groups + h)),
            pl.BlockSpec((1, t, vw), lambda i, h: (i, 0, v_blk + h)),
            pl.BlockSpec((1, t, vw), lambda i, h: (i, 0, v_blk + groups + h)),
            pl.BlockSpec((1, t, GLA_DV_PAD), lambda i, h: (i, 0, z_col // GLA_DV_PAD)),
            pl.BlockSpec((hs, GLA_DV_PAD, 2 * GLA_DK_PAD), lambda i, h: (h, 0, 0)),
            pl.BlockSpec((hs, 1, 2 * GLA_DK_PAD), lambda i, h: (h, 0, 0)),
            pl.BlockSpec((1, GLA_DV_PAD), lambda i, h: (0, 0)),
        ],
        out_specs=pl.BlockSpec((1, t, vw), lambda i, h: (i, 0, h)),
        scratch_shapes=[pltpu.VMEM((t, vw), F32), pltpu.VMEM((2 * hs, GLA_DV_PAD, GLA_DK_PAD), F32)],
        compiler_params=_compiler_params(("arbitrary", "arbitrary"), 56),
        name="gla",
    )(proj, proj, proj, proj, proj, wz, bz, g_norm_pad)


def _mix_out_router_kernel(mix_ref, qm_ref, mk_ref, mv_ref, wmix_ref, wmo_ref, h_ref, gain_ref, wr_hi_ref, wr_lo_ref,
                           h_out_ref, xn_ref, route_ref):
    tm = mix_ref.shape[1]
    low = lax.broadcasted_iota(I32, (tm, LANES), 1) < HEAD_DIM
    qm = qm_ref[0] * ATTN_SCALE
    h = h_ref[0] + _dot(mix_ref[0], wmix_ref[...])
    for p in range(MEM_HEADS // 2):
        lanes = slice(p * LANES, (p + 1) * LANES)
        o = _memory_attention_pair(qm[:, lanes], mk_ref[0, :, lanes], mv_ref[0, :, lanes], low)
        h = h + _dot(o.astype(BF16), wmo_ref[lanes, :])
    h_out_ref[0] = h
    xn = _rms(h, gain_ref[...])
    _rows_to_tiles(xn_ref, xn)

    x_hi = xn.astype(BF16)
    x_lo = (xn - x_hi.astype(F32)).astype(BF16)
    logits = _dot(x_hi, wr_hi_ref[...]) + (_dot(x_lo, wr_hi_ref[...]) + _dot(x_hi, wr_lo_ref[...]))
    lane = lax.broadcasted_iota(I32, (tm, LANES), 1)
    neg = jnp.float32(-jnp.inf)
    logits = jnp.where(lane < N_EXPERTS, logits, neg)
    m1 = jnp.max(logits, axis=-1, keepdims=True)
    i1 = jnp.min(jnp.where(logits == m1, lane, LANES), axis=-1, keepdims=True)
    rest = jnp.where(lane == i1, neg, logits)
    m2 = jnp.max(rest, axis=-1, keepdims=True)
    i2 = jnp.min(jnp.where(rest == m2, lane, LANES), axis=-1, keepdims=True)
    e2 = jnp.exp(m2 - m1)
    den = 1.0 + e2
    route = jnp.where(lane == 0, i1.astype(F32),
                      jnp.where(lane == 1, i2.astype(F32),
                                jnp.where(lane == 2, 1.0 / den, jnp.where(lane == 3, e2 / den, 0.0))))
    route_ref[0] = route


def _mix_out_router(mix, proj, mem_kv, w_mix, w_mo, h, gain, wr_hi, wr_lo):
    b, t, d = h.shape
    tm = ROW_TILE
    m = mem_kv.shape[1]
    qm_blk = (2 * GLA_HEADS * GLA_DK_PAD + 2 * GLA_HEADS * GLA_DV_PAD) // MEM_WIDTH
    const = lambda shape: pl.BlockSpec(shape, lambda i, j: (0,) * len(shape), pipeline_mode=pl.Buffered(1))
    tile = lambda w: pl.BlockSpec((1, tm, w), lambda i, j: (i, j, 0))
    return pl.pallas_call(
        _mix_out_router_kernel,
        out_shape=(jax.ShapeDtypeStruct((b, t, d), F32), jax.ShapeDtypeStruct((b * t * SUBLANES, LANES), F32),
                   jax.ShapeDtypeStruct((b, t, LANES), F32)),
        grid=(b, t // tm),
        in_specs=[
            tile(mix.shape[-1]),
            pl.BlockSpec((1, tm, MEM_WIDTH), lambda i, j: (i, j, qm_blk)),
            pl.BlockSpec((1, m, MEM_WIDTH), lambda i, j: (i, 0, 0)),
            pl.BlockSpec((1, m, MEM_WIDTH), lambda i, j: (i, 0, 1)),
            const(w_mix.shape), const(w_mo.shape),
            tile(d),
            const((1, d)), const(wr_hi.shape), const(wr_lo.shape),
        ],
        out_specs=(tile(d), pl.BlockSpec((tm * SUBLANES, LANES), lambda i, j: (i * (t // tm) + j, 0)), tile(LANES)),
        compiler_params=_compiler_params(("arbitrary", "arbitrary"), 40),
        name="mix_out_router",
    )(mix, proj, mem_kv, mem_kv, w_mix, w_mo, h, gain.reshape(1, d), wr_hi, wr_lo)


def _rows_to_tiles(ref, x):
    m = x.shape[0]
    for c in range(SUBLANES):
        ref[pl.ds(c, m, stride=SUBLANES), :] = x[:, c * LANES:(c + 1) * LANES]


def _tiles_to_rows(ref, m):
    return jnp.concatenate([ref[pl.ds(c, m, stride=SUBLANES), :] for c in range(SUBLANES)], axis=-1)


def _moe_kernel(be_ref, valid_ref, tok0_ref, tok_next_ref, dst_prev_ref, w_ref, x_hbm, wg_ref, wu_ref, wd_ref, y_hbm,
                rows_in, xb_ref, acc_ref, rows_out, sem, *, nf, n_out_rows):
    i = pl.program_id(0)
    f = pl.program_id(1)
    tm = xb_ref.shape[0]
    per_step = tm // nf
    used = valid_ref[i] > 0
    prev_used = jnp.logical_and(i > 0, valid_ref[jnp.maximum(i - 1, 0)] > 0)
    slot = lax.rem(i, 2)
    gather_sem, scatter_sem = sem.at[0], sem.at[1]

    def gather(tok_ref, j, buf):
        src = x_hbm.at[pl.ds(pl.multiple_of(tok_ref[0, 0, j], SUBLANES), SUBLANES)]
        return pltpu.make_async_copy(src, rows_in.at[buf, pl.ds(j * SUBLANES, SUBLANES)], gather_sem)

    def scatter(j):
        dst = y_hbm.at[pl.ds(pl.multiple_of(dst_prev_ref[0, 0, j], SUBLANES), SUBLANES)]
        return pltpu.make_async_copy(rows_out.at[pl.ds(j * SUBLANES, SUBLANES)], dst, scatter_sem)

    def wait_scatter(n_rows):
        pltpu.make_async_copy(rows_out.at[pl.ds(0, n_rows * SUBLANES)], y_hbm.at[pl.ds(0, n_rows * SUBLANES)],
                              scatter_sem).wait()

    @pl.when(jnp.logical_and(i == 0, f == 0))
    def _():
        rows_out[...] = jnp.zeros_like(rows_out)
        init = pltpu.make_async_copy(rows_out, y_hbm.at[pl.ds(n_out_rows * SUBLANES, tm * SUBLANES)], scatter_sem)
        init.start()
        init.wait()

        def start(j, carry):
            gather(tok0_ref, j, 0).start()
            return carry

        lax.fori_loop(0, tm, start, 0)

    @pl.when(jnp.logical_and(f == 0, jnp.logical_or(i == 0, prev_used)))
    def _():
        pltpu.make_async_copy(x_hbm.at[pl.ds(0, tm * SUBLANES)], rows_in.at[slot], gather_sem).wait()

    @pl.when(jnp.logical_and(f == 0, used))
    def _():
        xb_ref[...] = _tiles_to_rows(rows_in.at[slot], tm).astype(BF16)
        acc_ref[...] = jnp.zeros_like(acc_ref)

    @pl.when(used)
    def _():
        xb = xb_ref[...]
        act = (_silu(_dot(xb, wg_ref[0])) * _dot(xb, wu_ref[0])).astype(BF16)
        acc_ref[...] += _dot(act, wd_ref[0])
        for jj in range(per_step):
            j = f * per_step + jj
            gather(tok_next_ref, j, 1 - slot).start()
            scatter(j).start()

    @pl.when(jnp.logical_and(used, f == nf - 1))
    def _():
        wait_scatter(tm)
        _rows_to_tiles(rows_out, acc_ref[...] * w_ref[...])

    @pl.when(jnp.logical_and(f == 0, jnp.logical_and(prev_used, jnp.logical_not(used))))
    def _():
        n_valid = valid_ref[jnp.maximum(i - 1, 0)]

        def start(j, carry):
            scatter(j).start()
            return carry

        lax.fori_loop(0, n_valid, start, 0)
        wait_scatter(n_valid)


def _moe(x_tiles, block_expert, block_valid, tok_sorted, dst_sorted, w_sorted, w_gate, w_up, w_down):
    d = w_gate.shape[1]
    assert d == SUBLANES * LANES and x_tiles.shape[1] == LANES
    n = x_tiles.shape[0] // SUBLANES
    tm = MOE_ROWS
    nblk = tok_sorted.shape[0] // tm
    ff = w_gate.shape[-1]
    assert ff % MOE_FF_TILE == 0
    nf = ff // MOE_FF_TILE
    assert tm % nf == 0

    def f_idx(i, f, valid):
        return jnp.where(valid[i] > 0, f, nf - 1)

    tok_rows = (tok_sorted * SUBLANES).reshape(nblk, 1, tm)
    spare = (TOP_K * n + jnp.arange(tm, dtype=I32)) * SUBLANES
    dst_rows = jnp.concatenate([spare, dst_sorted * SUBLANES]).reshape(nblk + 1, 1, tm)
    smem_block = lambda index: pl.BlockSpec((1, 1, tm), index, memory_space=pltpu.SMEM)

    grid_spec = pltpu.PrefetchScalarGridSpec(
        num_scalar_prefetch=2,
        grid=(nblk, nf),
        in_specs=[
            smem_block(lambda i, f, be, valid: (0, 0, 0)),
            smem_block(lambda i, f, be, valid: (jnp.minimum(i + 1, nblk - 1), 0, 0)),
            smem_block(lambda i, f, be, valid: (i, 0, 0)),
            pl.BlockSpec((tm, 1), lambda i, f, be, valid: (i, 0)),
            pl.BlockSpec(memory_space=pl.ANY),
            pl.BlockSpec((1, d, MOE_FF_TILE), lambda i, f, be, valid: (be[i], 0, f_idx(i, f, valid))),
            pl.BlockSpec((1, d, MOE_FF_TILE), lambda i, f, be, valid: (be[i], 0, f_idx(i, f, valid))),
            pl.BlockSpec((1, MOE_FF_TILE, d), lambda i, f, be, valid: (be[i], f_idx(i, f, valid), 0)),
        ],
        out_specs=pl.BlockSpec(memory_space=pl.ANY),
        scratch_shapes=[pltpu.VMEM((2, tm * SUBLANES, LANES), F32), pltpu.VMEM((tm, d), BF16),
                        pltpu.VMEM((tm, d), F32), pltpu.VMEM((tm * SUBLANES, LANES), F32),
                        pltpu.SemaphoreType.DMA((2,))],
    )
    return pl.pallas_call(
        functools.partial(_moe_kernel, nf=nf, n_out_rows=TOP_K * n),
        grid_spec=grid_spec,
        out_shape=jax.ShapeDtypeStruct(((TOP_K * n + tm) * SUBLANES, LANES), F32),
        compiler_params=_compiler_params(("arbitrary", "arbitrary"), 56),
        name="moe",
    )(block_expert, block_valid, tok_rows, tok_rows, dst_rows, w_sorted.reshape(-1, 1), x_tiles, w_gate, w_up, w_down)


def _route_plan(route2d):
    n = route2d.shape[0]
    tm = MOE_ROWS
    n_slots = n * TOP_K
    cap = n_slots + N_EXPERTS * tm
    e_flat = route2d[:, :TOP_K].astype(I32).reshape(-1)
    w_flat = route2d[:, TOP_K:2 * TOP_K].reshape(-1)
    onehot = (e_flat[:, None] == jnp.arange(N_EXPERTS, dtype=I32)[None, :]).astype(I32)
    csum = jnp.cumsum(onehot, axis=0)
    rank = jnp.sum(csum * onehot, axis=1) - 1
    counts = csum[-1]
    padded = (counts + tm - 1) // tm * tm
    pend = jnp.cumsum(padded)
    pstart = pend - padded
    dest = jnp.sum(pstart[None, :] * onehot, axis=1) + rank
    pos = jnp.arange(cap, dtype=I32)
    slot_at = jnp.full((cap,), -1, I32).at[dest].set(jnp.arange(n_slots, dtype=I32), unique_indices=True)
    real = slot_at >= 0
    slot = jnp.maximum(slot_at, 0)
    tok_sorted = slot // TOP_K
    dst_sorted = jnp.where(real, (slot % TOP_K) * n + slot // TOP_K, n_slots + pos % tm)
    w_sorted = jnp.where(real, w_flat[slot], 0.0)
    block_start = jnp.arange(cap // tm, dtype=I32) * tm
    block_expert = jnp.clip(jnp.sum((block_start[:, None] >= pend[None, :]).astype(I32), axis=1), 0, N_EXPERTS - 1)
    real_end = (pstart + counts)[block_expert]
    block_valid = jnp.where(block_start < pend[-1], jnp.clip(real_end - block_start, 0, tm), 0).astype(I32)
    return tok_sorted, dst_sorted, w_sorted, block_expert, block_valid


def _final_kernel(h_ref, y0_ref, y1_ref, gain_ref, o_ref):
    tm = h_ref.shape[0]
    y = _tiles_to_rows(y0_ref, tm) + _tiles_to_rows(y1_ref, tm)
    o_ref[...] = _rms(h_ref[...] + y, gain_ref[...])


def _final(h2d, y_tiles, gain):
    n, d = h2d.shape
    tm = ROW_TILE
    y_block = lambda first: pl.BlockSpec((tm * SUBLANES, LANES), lambda i: (first + i, 0))
    return pl.pallas_call(
        _final_kernel,
        out_shape=jax.ShapeDtypeStruct((n, d), F32),
        grid=(n // tm,),
        in_specs=[
            pl.BlockSpec((tm, d), lambda i: (i, 0)),
            y_block(0), y_block(n // tm),
            pl.BlockSpec((1, d), lambda i: (0, 0)),
        ],
        out_specs=pl.BlockSpec((tm, d), lambda i: (i, 0)),
        compiler_params=_compiler_params(("arbitrary",), 32),
        name="final_norm",
    )(h2d, y_tiles, y_tiles, gain.reshape(1, d))


def _pad_heads(w, heads, width, padded):
    lead = w.shape[:-1]
    w = w.reshape(lead + (heads, width))
    w = jnp.pad(w, [(0, 0)] * len(lead) + [(0, 0), (0, padded - width)])
    return w.reshape(lead + (heads * padded,))


def _odd_layouts(w_in, w_gk_fwd, b_gk_fwd, w_gk_bwd, b_gk_bwd, g_norm, w_out):
    kw = GLA_HEADS * GLA_DK
    vw = GLA_HEADS * GLA_DV
    q, k, v, g, z, qm = jnp.split(w_in, [kw, 2 * kw, 2 * kw + vw, 2 * kw + 2 * vw, 2 * kw + 2 * vw + 2 * GLA_RANK], axis=-1)
    z = jnp.pad(z, ((0, 0), (0, GLA_DV_PAD - 2 * GLA_RANK)))
    w_in_pad = jnp.concatenate([
        _pad_heads(q, GLA_HEADS, GLA_DK, GLA_DK_PAD), _pad_heads(k, GLA_HEADS, GLA_DK, GLA_DK_PAD),
        _pad_heads(v, GLA_HEADS, GLA_DV, GLA_DV_PAD), _pad_heads(g, GLA_HEADS, GLA_DV, GLA_DV_PAD), qm, z], axis=-1)

    def per_head(a):
        return jnp.moveaxis(_pad_heads(a, GLA_HEADS, GLA_DK, GLA_DK_PAD).reshape(a.shape[0], GLA_HEADS, GLA_DK_PAD), 1, 0)

    wz = jnp.zeros((GLA_HEADS, GLA_DV_PAD, 2 * GLA_DK_PAD), F32)
    wz = wz.at[:, :GLA_RANK, :GLA_DK_PAD].set(per_head(w_gk_fwd))
    wz = wz.at[:, GLA_RANK:2 * GLA_RANK, GLA_DK_PAD:].set(per_head(w_gk_bwd))
    bz = jnp.concatenate([per_head(b_gk_fwd[None]), per_head(b_gk_bwd[None])], axis=-1)
    gn = jnp.pad(g_norm, (0, GLA_DV_PAD - GLA_DV)).reshape(1, GLA_DV_PAD)
    w_mix = _pad_heads(w_out[:vw].T, GLA_HEADS, GLA_DV, GLA_DV_PAD).T
    return w_in_pad.astype(BF16), wz.astype(BF16), bz.astype(F32), gn.astype(F32), w_mix.astype(BF16), w_out[vw:].astype(BF16)


def _router_split(w_router):
    w = jnp.pad(w_router.astype(F32), ((0, 0), (0, LANES - N_EXPERTS)))
    hi = w.astype(BF16)
    return hi, (w - hi.astype(F32)).astype(BF16)


def kernel(x, mem, even_norm1, even_w_in, even_rpb, even_w_out, even_norm2, even_w_gate, even_w_up, even_w_down, odd_norm1, odd_w_in, odd_w_gk_fwd, odd_b_gk_fwd, odd_w_gk_bwd, odd_b_gk_bwd, odd_g_norm, odd_w_out, odd_norm2, odd_w_router, odd_w_gate, odd_w_up, odd_w_down, mem_norm, w_mem_kv, final_norm):
    b, t, d = x.shape
    m = mem.shape[1]
    n = b * t
    assert even_norm1.shape[0] == 1 and odd_norm1.shape[0] == 1

    mem_kv = _norm_matmul(mem.reshape(b * m, d), mem_norm, w_mem_kv.astype(BF16)).reshape(b, m, 2 * MEM_WIDTH)

    proj = _norm_matmul(x.reshape(n, d), even_norm1[0], even_w_in[0].astype(BF16)).reshape(b, t, -1)
    h = _na_mixer(proj, mem_kv, _na_bias_table(even_rpb[0]), even_w_out[0].astype(BF16), x, rows_per_step=8)
    h = _ffn(h.reshape(n, d), even_norm2[0], even_w_gate[0].astype(BF16), even_w_up[0].astype(BF16),
             even_w_down[0].astype(BF16)).reshape(b, t, d)

    w_in, wz, bz, gn, w_mix, w_mo = _odd_layouts(odd_w_in[0], odd_w_gk_fwd[0], odd_b_gk_fwd[0], odd_w_gk_bwd[0],
                                                odd_b_gk_bwd[0], odd_g_norm[0], odd_w_out[0])
    proj = _norm_matmul(h.reshape(n, d), odd_norm1[0], w_in).reshape(b, t, -1)
    mix = _gla(proj, wz, bz, gn)
    wr_hi, wr_lo = _router_split(odd_w_router[0])
    h, xn_tiles, route = _mix_out_router(mix, proj, mem_kv, w_mix, w_mo, h, odd_norm2[0], wr_hi, wr_lo)
    tok_sorted, dst_sorted, w_sorted, block_expert, block_valid = _route_plan(route.reshape(n, LANES))
    y_tiles = _moe(xn_tiles, block_expert, block_valid, tok_sorted, dst_sorted, w_sorted,
                   odd_w_gate[0].astype(BF16), odd_w_up[0].astype(BF16), odd_w_down[0].astype(BF16))
    return _final(h.reshape(n, d), y_tiles, final_norm).reshape(b, t, d)
```

```python
import functools

import numpy as np
import jax
import jax.numpy as jnp
from jax import lax
from jax.experimental import pallas as pl
from jax.experimental.pallas import tpu as pltpu

F32 = jnp.float32
BF16 = jnp.bfloat16
I32 = jnp.int32

EPS = 1e-6
GRID_W = 64
HEAD_DIM = 64
ATTN_SCALE = HEAD_DIM ** -0.5
NA_HEADS = 12
NA_WIDTH = NA_HEADS * HEAD_DIM
NA_WIN_ROWS = 8
NA_WIN_COLS = 16
MEM_HEADS = 4
MEM_WIDTH = MEM_HEADS * HEAD_DIM
GLA_HEADS = 4
GLA_DK = 96
GLA_DV = 192
GLA_DK_PAD = 128
GLA_DV_PAD = 256
GLA_RANK = 16
GLA_GATE_NORM = 16.0
GLA_CHUNK = 64
GLA_BLOCK_CHUNKS = 4
GLA_HEADS_PER_STEP = 2
N_EXPERTS = 8
TOP_K = 2
LANES = 128
SUBLANES = 8
V7X_VMEM_BYTES = 64 * 1024 * 1024

ROW_TILE = 512
MOE_ROWS = 512
MOE_FF_TILE = 1792
MOE_DMA_UNROLL = 16
PROJ_COL_CHUNK = 512


def _compiler_params(semantics, vmem_mib):
    assert vmem_mib * 1024 * 1024 < V7X_VMEM_BYTES
    return pltpu.CompilerParams(dimension_semantics=semantics, vmem_limit_bytes=vmem_mib * 1024 * 1024)


def _rms(x, gain):
    return x * lax.rsqrt(jnp.mean(x * x, axis=-1, keepdims=True) + EPS) * gain


def _dot(a, b):
    return jnp.dot(a, b, preferred_element_type=F32)


def _dot_nt(a, b):
    return lax.dot_general(a, b, (((1,), (1,)), ((), ())), preferred_element_type=F32)


def _dot_tn(a, b):
    return lax.dot_general(a, b, (((0,), (0,)), ((), ())), preferred_element_type=F32)


def _silu(x):
    return x * jax.nn.sigmoid(x)


def _norm_matmul_kernel(x_ref, gain_ref, w_ref, o_ref):
    y = _rms(x_ref[...], gain_ref[...]).astype(BF16)
    for j in range(0, o_ref.shape[-1], PROJ_COL_CHUNK):
        o_ref[:, j:j + PROJ_COL_CHUNK] = _dot(y, w_ref[:, j:j + PROJ_COL_CHUNK]).astype(o_ref.dtype)


def _norm_matmul(x2d, gain, w):
    n, d = x2d.shape
    nout = w.shape[1]
    assert n % ROW_TILE == 0 and nout % PROJ_COL_CHUNK == 0
    return pl.pallas_call(
        _norm_matmul_kernel,
        out_shape=jax.ShapeDtypeStruct((n, nout), BF16),
        grid=(n // ROW_TILE,),
        in_specs=[
            pl.BlockSpec((ROW_TILE, d), lambda i: (i, 0)),
            pl.BlockSpec((1, d), lambda i: (0, 0)),
            pl.BlockSpec((d, nout), lambda i: (0, 0), pipeline_mode=pl.Buffered(1)),
        ],
        out_specs=pl.BlockSpec((ROW_TILE, nout), lambda i: (i, 0)),
        compiler_params=_compiler_params(("arbitrary",), 40),
        name="norm_matmul",
    )(x2d, gain.reshape(1, d), w)


def _softmax_pv(s, v_pair):
    m = jnp.max(s, axis=-1, keepdims=True)
    e = jnp.exp(s - m)
    l = jnp.sum(e, axis=-1, keepdims=True)
    return _dot(e.astype(BF16), v_pair) / l


def _memory_attention_pair(q_pair, k_pair, v_pair, low_half):
    zero = jnp.zeros_like(q_pair)
    outs = []
    for sel in (low_half, jnp.logical_not(low_half)):
        s = _dot_nt(jnp.where(sel, q_pair, zero), k_pair)
        outs.append(_softmax_pv(s, v_pair))
    return jnp.where(low_half, outs[0], outs[1])


def _na_mixer_kernel(q_ref, k_ref, v_ref, qm_ref, mk_ref, mv_ref, bias_ref, wo_ref, h_ref, o_ref, mix_ref,
                     *, rows_per_step, n_rows):
    g = pl.program_id(1)
    w = GRID_W
    win = NA_WIN_ROWS
    band_rows = win + 1
    band = band_rows * w
    n_pairs = NA_HEADS // 2
    masked_slab = 2 * win - 1
    low_row = lax.broadcasted_iota(I32, (w, LANES), 1) < HEAD_DIM
    high_row = jnp.logical_not(low_row)
    zero = jnp.zeros((w, LANES), BF16)

    def rows_body(pp, carry):
        local = [2 * pp, 2 * pp + 1]
        rows = [g * rows_per_step + r for r in local]
        starts = [jnp.clip(r - win // 2, 0, n_rows - win) for r in rows]
        band0 = jnp.minimum(starts[0], n_rows - band_rows)
        k_off = pl.multiple_of(band0 * w, w)
        slab = [[jnp.where(jnp.logical_and(band0 + a >= starts[x], band0 + a < starts[x] + win),
                           band0 + a - rows[x] + win - 1, masked_slab)
                 for a in range(band_rows)] for x in range(2)]
        q_rows = [q_ref[0, pl.ds(pl.multiple_of(r * w, w), w), :] * ATTN_SCALE for r in local]
        lanes = [slice(p * LANES, (p + 1) * LANES) for p in range(n_pairs)]
        s = []
        for p in range(n_pairs):
            q_t = jnp.concatenate([jnp.where(sel, q_rows[x][:, lanes[p]], zero)
                                   for x in range(2) for sel in (low_row, high_row)], axis=0)
            s.append(_dot_nt(k_ref[0, pl.ds(k_off, band), lanes[p]], q_t))
        for p in range(n_pairs):
            bias = jnp.concatenate(
                [jnp.concatenate([bias_ref[p, slab[0][a]], bias_ref[p, slab[1][a]]], axis=-1)
                 for a in range(band_rows)], axis=0)
            s[p] = s[p] + bias
        e = [jnp.exp(s[p] - jnp.max(s[p], axis=0, keepdims=True)) for p in range(n_pairs)]
        prob = [(e[p] * (1.0 / jnp.sum(e[p], axis=0, keepdims=True))).astype(BF16) for p in range(n_pairs)]
        o = [_dot_tn(prob[p], v_ref[0, pl.ds(k_off, band), lanes[p]]) for p in range(n_pairs)]
        for p in range(n_pairs):
            for x in range(2):
                ox = jnp.where(low_row, o[p][(2 * x) * w:(2 * x + 1) * w], o[p][(2 * x + 1) * w:(2 * x + 2) * w])
                mix_ref[pl.ds(pl.multiple_of(local[x] * w, w), w), lanes[p]] = ox.astype(BF16)
        return carry

    lax.fori_loop(0, rows_per_step // 2, rows_body, 0)

    tm = rows_per_step * GRID_W
    low = lax.broadcasted_iota(I32, (tm, LANES), 1) < HEAD_DIM
    qm = qm_ref[0] * ATTN_SCALE
    for p in range(MEM_HEADS // 2):
        lanes = slice(p * LANES, (p + 1) * LANES)
        o = _memory_attention_pair(qm[:, lanes], mk_ref[0, :, lanes], mv_ref[0, :, lanes], low)
        mix_ref[:, NA_WIDTH + p * LANES:NA_WIDTH + (p + 1) * LANES] = o.astype(BF16)
    o_ref[0] = h_ref[0] + _dot(mix_ref[...], wo_ref[...])


def _na_bias_table(rpb):
    qc = np.arange(GRID_W)[None, :]
    kc = np.arange(GRID_W)[:, None]
    cs = np.clip(qc - NA_WIN_COLS // 2, 0, GRID_W - NA_WIN_COLS)
    ok = (kc >= cs) & (kc < cs + NA_WIN_COLS)
    dc = np.clip(kc - qc, -(NA_WIN_COLS - 1), NA_WIN_COLS - 1) + NA_WIN_COLS - 1
    t = jnp.where(jnp.asarray(ok)[None, None], rpb[:, :, dc].astype(F32), -jnp.inf)
    t = jnp.concatenate([t, jnp.full((NA_HEADS, 1, GRID_W, GRID_W), -jnp.inf, F32)], axis=1)
    return jnp.concatenate([t[0::2], t[1::2]], axis=-1)


def _na_mixer(proj, mem_kv, bias_tab, w_out, h, rows_per_step):
    b, t, d = h.shape
    n_rows = t // GRID_W
    assert n_rows >= NA_WIN_ROWS and n_rows % rows_per_step == 0
    tm = rows_per_step * GRID_W
    m = mem_kv.shape[1]
    kernel = functools.partial(_na_mixer_kernel, rows_per_step=rows_per_step, n_rows=n_rows)
    kv_blocks = NA_WIDTH // MEM_WIDTH
    return pl.pallas_call(
        kernel,
        out_shape=jax.ShapeDtypeStruct((b, t, d), F32),
        grid=(b, n_rows // rows_per_step),
        in_specs=[
            pl.BlockSpec((1, tm, NA_WIDTH), lambda i, j: (i, j, 0)),
            pl.BlockSpec((1, t, NA_WIDTH), lambda i, j: (i, 0, 1)),
            pl.BlockSpec((1, t, NA_WIDTH), lambda i, j: (i, 0, 2)),
            pl.BlockSpec((1, tm, MEM_WIDTH), lambda i, j: (i, j, 3 * kv_blocks)),
            pl.BlockSpec((1, m, MEM_WIDTH), lambda i, j: (i, 0, 0)),
            pl.BlockSpec((1, m, MEM_WIDTH), lambda i, j: (i, 0, 1)),
            pl.BlockSpec(bias_tab.shape, lambda i, j: (0, 0, 0, 0), pipeline_mode=pl.Buffered(1)),
            pl.BlockSpec(w_out.shape, lambda i, j: (0, 0), pipeline_mode=pl.Buffered(1)),
            pl.BlockSpec((1, tm, d), lambda i, j: (i, j, 0)),
        ],
        out_specs=pl.BlockSpec((1, tm, d), lambda i, j: (i, j, 0)),
        scratch_shapes=[pltpu.VMEM((tm, d), BF16)],
        compiler_params=_compiler_params(("arbitrary", "arbitrary"), 56),
        name="na_mixer",
    )(proj, proj, proj, proj, mem_kv, mem_kv, bias_tab, w_out, h)


def _ffn_kernel(x_ref, gain_ref, wg_ref, wu_ref, wd_ref, o_ref):
    x = x_ref[...]
    xn = _rms(x, gain_ref[...]).astype(BF16)
    act = (_silu(_dot(xn, wg_ref[...])) * _dot(xn, wu_ref[...])).astype(BF16)
    o_ref[...] = x + _dot(act, wd_ref[...])


def _ffn(x2d, gain, w_gate, w_up, w_down):
    n, d = x2d.shape
    ff = w_gate.shape[1]
    assert n % ROW_TILE == 0
    resident = lambda shape: pl.BlockSpec(shape, lambda i: (0, 0), pipeline_mode=pl.Buffered(1))
    return pl.pallas_call(
        _ffn_kernel,
        out_shape=jax.ShapeDtypeStruct((n, d), F32),
        grid=(n // ROW_TILE,),
        in_specs=[
            pl.BlockSpec((ROW_TILE, d), lambda i: (i, 0)),
            resident((1, d)), resident((d, ff)), resident((d, ff)), resident((ff, d)),
        ],
        out_specs=pl.BlockSpec((ROW_TILE, d), lambda i: (i, 0)),
        compiler_params=_compiler_params(("arbitrary",), 56),
        name="ffn",
    )(x2d, gain.reshape(1, d), w_gate, w_up, w_down)


def _split3(x):
    x1 = x.astype(BF16)
    r1 = x - x1.astype(F32)
    x2 = r1.astype(BF16)
    x3 = (r1 - x2.astype(F32)).astype(BF16)
    return x1, x2, x3


def _gla_kernel(q_ref, k_ref, v_ref, gate_ref, z_ref, wz_ref, bz_ref, gn_ref, o_ref, part_ref, st_ref, *, n_blocks):
    c = GLA_CHUNK
    r = GLA_BLOCK_CHUNKS * c
    row = lax.broadcasted_iota(I32, (r, r), 0)
    col = lax.broadcasted_iota(I32, (r, r), 1)
    same = jnp.where(lax.shift_right_logical(row, 6) == lax.shift_right_logical(col, 6), 1.0, 0.0)
    assert c == 1 << 6
    lower = jnp.where(row >= col, same, 0.0)
    upper_incl = jnp.where(row <= col, same, 0.0)
    upper = jnp.where(row < col, same, 0.0)
    sum_ops = (lower.astype(BF16), upper_incl.astype(BF16))
    keeps = (lower > 0.0, upper > 0.0)
    q_scale = GLA_DK ** -0.5

    chains = [(hh, d) for hh in range(GLA_HEADS_PER_STEP) for d in (0, 1)]

    def blocks(r0s):
        z, q, k, v = [], [], [], []
        for hh, d in chains:
            dk = slice(hh * GLA_DK_PAD, (hh + 1) * GLA_DK_PAD)
            dv = slice(hh * GLA_DV_PAD, (hh + 1) * GLA_DV_PAD)
            gl = slice(d * GLA_DK_PAD, (d + 1) * GLA_DK_PAD)
            z.append(_dot(z_ref[0, pl.ds(r0s[d], r), :], wz_ref[hh, :, gl]) + bz_ref[hh, :, gl])
            q.append(q_ref[0, pl.ds(r0s[d], r), dk].astype(F32) * q_scale)
            k.append(k_ref[0, pl.ds(r0s[d], r), dk].astype(F32))
            v.append(v_ref[0, pl.ds(r0s[d], r), dv])
        n = range(len(chains))
        gs = [_split3((jnp.minimum(z[i], 0.0) - jnp.log1p(jnp.exp(-jnp.abs(z[i])))) * (1.0 / GLA_GATE_NORM)) for i in n]
        b = [_dot(sum_ops[chains[i][1]], jnp.concatenate(gs[i], axis=-1)) for i in n]
        b = [x[:, :GLA_DK_PAD] + x[:, GLA_DK_PAD:2 * GLA_DK_PAD] + x[:, 2 * GLA_DK_PAD:] for x in b]
        ends = [[x[ci * c + (0 if d else c - 1):ci * c + (1 if d else c)] for ci in range(GLA_BLOCK_CHUNKS)]
                for x, (_, d) in zip(b, chains)]
        b_end = [jnp.concatenate([jnp.broadcast_to(e, (c, GLA_DK_PAD)) for e in es], axis=0) for es in ends]
        q_t = [(q[i] * jnp.exp(b[i])).astype(BF16) for i in n]
        k_t = [(k[i] * jnp.exp(-b[i])).astype(BF16) for i in n]
        a = [jnp.where(keeps[chains[i][1]], _dot_nt(q_t[i], k_t[i]), 0.0).astype(BF16) for i in n]
        k_d = [(k[i] * jnp.exp(b_end[i] - b[i])).astype(BF16) for i in n]
        decay = [[jnp.exp(e) for e in es] for es in ends]
        intra = [_dot(a[i], v[i]) for i in n]
        state = [st_ref[i] for i in n]
        outs = [[None] * GLA_BLOCK_CHUNKS for _ in n]
        for step in range(GLA_BLOCK_CHUNKS):
            for i, (_, d) in enumerate(chains):
                ci = GLA_BLOCK_CHUNKS - 1 - step if d else step
                rows = slice(ci * c, (ci + 1) * c)
                outs[i][ci] = intra[i][rows] + _dot_nt(q_t[i][rows], state[i].astype(BF16))
                state[i] = state[i] * decay[i][ci] + _dot_tn(v[i][rows], k_d[i][rows])
        for i in n:
            st_ref[i] = state[i]
        return [jnp.concatenate(o, axis=0) for o in outs]

    def park(r0s, outs):
        for (hh, d), o in zip(chains, outs):
            part_ref[pl.ds(r0s[d], r), hh * GLA_DV_PAD:(hh + 1) * GLA_DV_PAD] = o

    def finish(r0s, outs):
        for (hh, d), o in zip(chains, outs):
            dv = slice(hh * GLA_DV_PAD, (hh + 1) * GLA_DV_PAD)
            o = o + part_ref[pl.ds(r0s[d], r), dv]
            ms = jnp.sum(o * o, axis=-1, keepdims=True) * (1.0 / GLA_DV)
            y = o * lax.rsqrt(ms + EPS) * gn_ref[...]
            o_ref[0, pl.ds(r0s[d], r), dv] = (y * _silu(gate_ref[0, pl.ds(r0s[d], r), dv].astype(F32))).astype(BF16)

    st_ref[...] = jnp.zeros_like(st_ref)

    def trip(epilogue, j, carry):
        r0s = (pl.multiple_of(j * r, r), pl.multiple_of((n_blocks - 1 - j) * r, r))
        epilogue(r0s, blocks(r0s))
        return carry

    lax.fori_loop(0, n_blocks // 2, functools.partial(trip, park), 0)
    lax.fori_loop(n_blocks // 2, n_blocks, functools.partial(trip, finish), 0)


def _gla(proj, wz, bz, g_norm_pad):
    b, t, _ = proj.shape
    rows = GLA_BLOCK_CHUNKS * GLA_CHUNK
    assert t % (2 * rows) == 0
    kernel = functools.partial(_gla_kernel, n_blocks=t // rows)
    hs = GLA_HEADS_PER_STEP
    groups = GLA_HEADS // hs
    kw, vw = hs * GLA_DK_PAD, hs * GLA_DV_PAD
    v_blk = 2 * groups * kw // vw
    z_col = 2 * GLA_HEADS * (GLA_DK_PAD + GLA_DV_PAD) + MEM_WIDTH
    return pl.pallas_call(
        kernel,
        out_shape=jax.ShapeDtypeStruct((b, t, GLA_HEADS * GLA_DV_PAD), BF16),
        grid=(b, groups),
        in_specs=[
            pl.BlockSpec((1, t, kw), lambda i, h: (i, 0, h)),
            pl.BlockSpec((1, t, kw), lambda i, h: (i, 0, groups + h)),
            pl.BlockSpec((1, t, vw), lambda i, h: (i, 0, v_blk + h)),
            pl.BlockSpec((1, t, vw), lambda i, h: (i, 0, v_blk + groups + h)),
            pl.BlockSpec((1, t, GLA_DV_PAD), lambda i, h: (i, 0, z_col // GLA_DV_PAD)),
            pl.BlockSpec((hs, GLA_DV_PAD, 2 * GLA_DK_PAD), lambda i, h: (h, 0, 0)),
            pl.BlockSpec((hs, 1, 2 * GLA_DK_PAD), lambda i, h: (h, 0, 0)),
            pl.BlockSpec((1, GLA_DV_PAD), lambda i, h: (0, 0)),
        ],
        out_specs=pl.BlockSpec((1, t, vw), lambda i, h: (i, 0, h)),
        scratch_shapes=[pltpu.VMEM((t, vw), F32), pltpu.VMEM((2 * hs, GLA_DV_PAD, GLA_DK_PAD), F32)],
        compiler_params=_compiler_params(("arbitrary", "arbitrary"), 56),
        name="gla",
    )(proj, proj, proj, proj, proj, wz, bz, g_norm_pad)


def _mix_out_router_kernel(mix_ref, qm_ref, mk_ref, mv_ref, wmix_ref, wmo_ref, h_ref, gain_ref, wr_hi_ref, wr_lo_ref,
                           h_out_ref, xn_ref, route_ref):
    tm = mix_ref.shape[1]
    low = lax.broadcasted_iota(I32, (tm, LANES), 1) < HEAD_DIM
    qm = qm_ref[0] * ATTN_SCALE
    h = h_ref[0] + _dot(mix_ref[0], wmix_ref[...])
    for p in range(MEM_HEADS // 2):
        lanes = slice(p * LANES, (p + 1) * LANES)
        o = _memory_attention_pair(qm[:, lanes], mk_ref[0, :, lanes], mv_ref[0, :, lanes], low)
        h = h + _dot(o.astype(BF16), wmo_ref[lanes, :])
    h_out_ref[0] = h
    xn = _rms(h, gain_ref[...])
    _rows_to_tiles(xn_ref, xn)

    x_hi = xn.astype(BF16)
    x_lo = (xn - x_hi.astype(F32)).astype(BF16)
    logits = _dot(x_hi, wr_hi_ref[...]) + (_dot(x_lo, wr_hi_ref[...]) + _dot(x_hi, wr_lo_ref[...]))
    lane = lax.broadcasted_iota(I32, (tm, LANES), 1)
    neg = jnp.float32(-jnp.inf)
    logits = jnp.where(lane < N_EXPERTS, logits, neg)
    m1 = jnp.max(logits, axis=-1, keepdims=True)
    i1 = jnp.min(jnp.where(logits == m1, lane, LANES), axis=-1, keepdims=True)
    rest = jnp.where(lane == i1, neg, logits)
    m2 = jnp.max(rest, axis=-1, keepdims=True)
    i2 = jnp.min(jnp.where(rest == m2, lane, LANES), axis=-1, keepdims=True)
    e2 = jnp.exp(m2 - m1)
    den = 1.0 + e2
    route = jnp.where(lane == 0, i1.astype(F32),
                      jnp.where(lane == 1, i2.astype(F32),
                                jnp.where(lane == 2, 1.0 / den, jnp.where(lane == 3, e2 / den, 0.0))))
    route_ref[0] = route


def _mix_out_router(mix, proj, mem_kv, w_mix, w_mo, h, gain, wr_hi, wr_lo):
    b, t, d = h.shape
    tm = ROW_TILE
    m = mem_kv.shape[1]
    qm_blk = (2 * GLA_HEADS * GLA_DK_PAD + 2 * GLA_HEADS * GLA_DV_PAD) // MEM_WIDTH
    const = lambda shape: pl.BlockSpec(shape, lambda i, j: (0,) * len(shape), pipeline_mode=pl.Buffered(1))
    tile = lambda w: pl.BlockSpec((1, tm, w), lambda i, j: (i, j, 0))
    return pl.pallas_call(
        _mix_out_router_kernel,
        out_shape=(jax.ShapeDtypeStruct((b, t, d), F32), jax.ShapeDtypeStruct((b * t * SUBLANES, LANES), F32),
                   jax.ShapeDtypeStruct((b, t, LANES), F32)),
        grid=(b, t // tm),
        in_specs=[
            tile(mix.shape[-1]),
            pl.BlockSpec((1, tm, MEM_WIDTH), lambda i, j: (i, j, qm_blk)),
            pl.BlockSpec((1, m, MEM_WIDTH), lambda i, j: (i, 0, 0)),
            pl.BlockSpec((1, m, MEM_WIDTH), lambda i, j: (i, 0, 1)),
            const(w_mix.shape), const(w_mo.shape),
            tile(d),
            const((1, d)), const(wr_hi.shape), const(wr_lo.shape),
        ],
        out_specs=(tile(d), pl.BlockSpec((tm * SUBLANES, LANES), lambda i, j: (i * (t // tm) + j, 0)), tile(LANES)),
        compiler_params=_compiler_params(("arbitrary", "arbitrary"), 40),
        name="mix_out_router",
    )(mix, proj, mem_kv, mem_kv, w_mix, w_mo, h, gain.reshape(1, d), wr_hi, wr_lo)


def _rows_to_tiles(ref, x):
    m = x.shape[0]
    for c in range(SUBLANES):
        ref[pl.ds(c, m, stride=SUBLANES), :] = x[:, c * LANES:(c + 1) * LANES]


def _tiles_to_rows(ref, m):
    return jnp.concatenate([ref[pl.ds(c, m, stride=SUBLANES), :] for c in range(SUBLANES)], axis=-1)


def _moe_kernel(be_ref, valid_ref, tok0_ref, tok_next_ref, dst_prev_ref, w_ref, x_hbm, wg_ref, wu_ref, wd_ref, y_hbm,
                rows_in, xb_ref, acc_ref, rows_out, sem, *, nf, n_out_rows):
    i = pl.program_id(0)
    f = pl.program_id(1)
    tm = xb_ref.shape[0]
    per_step = tm // nf
    used = valid_ref[i] > 0
    prev_used = jnp.logical_and(i > 0, valid_ref[jnp.maximum(i - 1, 0)] > 0)
    slot = lax.rem(i, 2)
    gather_sem, scatter_sem = sem.at[0], sem.at[1]

    def gather(tok_ref, j, buf):
        src = x_hbm.at[pl.ds(pl.multiple_of(tok_ref[0, 0, j], SUBLANES), SUBLANES)]
        return pltpu.make_async_copy(src, rows_in.at[buf, pl.ds(j * SUBLANES, SUBLANES)], gather_sem)

    def scatter(j):
        dst = y_hbm.at[pl.ds(pl.multiple_of(dst_prev_ref[0, 0, j], SUBLANES), SUBLANES)]
        return pltpu.make_async_copy(rows_out.at[pl.ds(j * SUBLANES, SUBLANES)], dst, scatter_sem)

    def wait_scatter(n_rows):
        pltpu.make_async_copy(rows_out.at[pl.ds(0, n_rows * SUBLANES)], y_hbm.at[pl.ds(0, n_rows * SUBLANES)],
                              scatter_sem).wait()

    @pl.when(jnp.logical_and(i == 0, f == 0))
    def _():
        rows_out[...] = jnp.zeros_like(rows_out)
        init = pltpu.make_async_copy(rows_out, y_hbm.at[pl.ds(n_out_rows * SUBLANES, tm * SUBLANES)], scatter_sem)
        init.start()
        init.wait()

        def start(j, carry):
            gather(tok0_ref, j, 0).start()
            return carry

        lax.fori_loop(0, tm, start, 0)

    @pl.when(jnp.logical_and(f == 0, jnp.logical_or(i == 0, prev_used)))
    def _():
        pltpu.make_async_copy(x_hbm.at[pl.ds(0, tm * SUBLANES)], rows_in.at[slot], gather_sem).wait()

    @pl.when(jnp.logical_and(f == 0, used))
    def _():
        xb_ref[...] = _tiles_to_rows(rows_in.at[slot], tm).astype(BF16)
        acc_ref[...] = jnp.zeros_like(acc_ref)

    @pl.when(used)
    def _():
        def issue(jo, carry):
            for jj in range(MOE_DMA_UNROLL):
                j = f * per_step + jo * MOE_DMA_UNROLL + jj
                gather(tok_next_ref, j, 1 - slot).start()
                scatter(j).start()
            return carry

        lax.fori_loop(0, per_step // MOE_DMA_UNROLL, issue, 0)
        xb = xb_ref[...]
        act = (_silu(_dot(xb, wg_ref[0])) * _dot(xb, wu_ref[0])).astype(BF16)
        acc_ref[...] += _dot(act, wd_ref[0])

    @pl.when(jnp.logical_and(used, f == nf - 1))
    def _():
        wait_scatter(tm)
        _rows_to_tiles(rows_out, acc_ref[...] * w_ref[...])

    @pl.when(jnp.logical_and(f == 0, jnp.logical_and(prev_used, jnp.logical_not(used))))
    def _():
        n_valid = valid_ref[jnp.maximum(i - 1, 0)]

        def start(j, carry):
            scatter(j).start()
            return carry

        lax.fori_loop(0, n_valid, start, 0)
        wait_scatter(n_valid)


def _moe(x_tiles, block_expert, block_valid, tok_sorted, dst_sorted, w_sorted, w_gate, w_up, w_down):
    d = w_gate.shape[1]
    assert d == SUBLANES * LANES and x_tiles.shape[1] == LANES
    n = x_tiles.shape[0] // SUBLANES
    tm = MOE_ROWS
    nblk = tok_sorted.shape[0] // tm
    ff = w_gate.shape[-1]
    assert ff % MOE_FF_TILE == 0
    nf = ff // MOE_FF_TILE
    assert tm % nf == 0

    def f_idx(i, f, valid):
        return jnp.where(valid[i] > 0, f, nf - 1)

    tok_rows = (tok_sorted * SUBLANES).reshape(nblk, 1, tm)
    spare = (TOP_K * n + jnp.arange(tm, dtype=I32)) * SUBLANES
    dst_rows = jnp.concatenate([spare, dst_sorted * SUBLANES]).reshape(nblk + 1, 1, tm)
    smem_block = lambda index: pl.BlockSpec((1, 1, tm), index, memory_space=pltpu.SMEM)

    grid_spec = pltpu.PrefetchScalarGridSpec(
        num_scalar_prefetch=2,
        grid=(nblk, nf),
        in_specs=[
            smem_block(lambda i, f, be, valid: (0, 0, 0)),
            smem_block(lambda i, f, be, valid: (jnp.minimum(i + 1, nblk - 1), 0, 0)),
            smem_block(lambda i, f, be, valid: (i, 0, 0)),
            pl.BlockSpec((tm, 1), lambda i, f, be, valid: (i, 0)),
            pl.BlockSpec(memory_space=pl.ANY),
            pl.BlockSpec((1, d, MOE_FF_TILE), lambda i, f, be, valid: (be[i], 0, f_idx(i, f, valid))),
            pl.BlockSpec((1, d, MOE_FF_TILE), lambda i, f, be, valid: (be[i], 0, f_idx(i, f, valid))),
            pl.BlockSpec((1, MOE_FF_TILE, d), lambda i, f, be, valid: (be[i], f_idx(i, f, valid), 0)),
        ],
        out_specs=pl.BlockSpec(memory_space=pl.ANY),
        scratch_shapes=[pltpu.VMEM((2, tm * SUBLANES, LANES), F32), pltpu.VMEM((tm, d), BF16),
                        pltpu.VMEM((tm, d), F32), pltpu.VMEM((tm * SUBLANES, LANES), F32),
                        pltpu.SemaphoreType.DMA((2,))],
    )
    return pl.pallas_call(
        functools.partial(_moe_kernel, nf=nf, n_out_rows=TOP_K * n),
        grid_spec=grid_spec,
        out_shape=jax.ShapeDtypeStruct(((TOP_K * n + tm) * SUBLANES, LANES), F32),
        compiler_params=_compiler_params(("arbitrary", "arbitrary"), 56),
        name="moe",
    )(block_expert, block_valid, tok_rows, tok_rows, dst_rows, w_sorted.reshape(-1, 1), x_tiles, w_gate, w_up, w_down)


def _route_plan(route2d):
    n = route2d.shape[0]
    tm = MOE_ROWS
    n_slots = n * TOP_K
    cap = n_slots + N_EXPERTS * tm
    e_flat = route2d[:, :TOP_K].astype(I32).reshape(-1)
    w_flat = route2d[:, TOP_K:2 * TOP_K].reshape(-1)
    onehot = (e_flat[:, None] == jnp.arange(N_EXPERTS, dtype=I32)[None, :]).astype(I32)
    csum = jnp.cumsum(onehot, axis=0)
    rank = jnp.sum(csum * onehot, axis=1) - 1
    counts = csum[-1]
    padded = (counts + tm - 1) // tm * tm
    pend = jnp.cumsum(padded)
    pstart = pend - padded
    dest = jnp.sum(pstart[None, :] * onehot, axis=1) + rank
    pos = jnp.arange(cap, dtype=I32)
    slot_at = jnp.full((cap,), -1, I32).at[dest].set(jnp.arange(n_slots, dtype=I32), unique_indices=True)
    real = slot_at >= 0
    slot = jnp.maximum(slot_at, 0)
    tok_sorted = slot // TOP_K
    dst_sorted = jnp.where(real, (slot % TOP_K) * n + slot // TOP_K, n_slots + pos % tm)
    w_sorted = jnp.where(real, w_flat[slot], 0.0)
    block_start = jnp.arange(cap // tm, dtype=I32) * tm
    block_expert = jnp.clip(jnp.sum((block_start[:, None] >= pend[None, :]).astype(I32), axis=1), 0, N_EXPERTS - 1)
    real_end = (pstart + counts)[block_expert]
    block_valid = jnp.where(block_start < pend[-1], jnp.clip(real_end - block_start, 0, tm), 0).astype(I32)
    return tok_sorted, dst_sorted, w_sorted, block_expert, block_valid


def _final_kernel(h_ref, y0_ref, y1_ref, gain_ref, o_ref):
    tm = h_ref.shape[0]
    y = _tiles_to_rows(y0_ref, tm) + _tiles_to_rows(y1_ref, tm)
    o_ref[...] = _rms(h_ref[...] + y, gain_ref[...])


def _final(h2d, y_tiles, gain):
    n, d = h2d.shape
    tm = ROW_TILE
    y_block = lambda first: pl.BlockSpec((tm * SUBLANES, LANES), lambda i: (first + i, 0))
    return pl.pallas_call(
        _final_kernel,
        out_shape=jax.ShapeDtypeStruct((n, d), F32),
        grid=(n // tm,),
        in_specs=[
            pl.BlockSpec((tm, d), lambda i: (i, 0)),
            y_block(0), y_block(n // tm),
            pl.BlockSpec((1, d), lambda i: (0, 0)),
        ],
        out_specs=pl.BlockSpec((tm, d), lambda i: (i, 0)),
        compiler_params=_compiler_params(("arbitrary",), 32),
        name="final_norm",
    )(h2d, y_tiles, y_tiles, gain.reshape(1, d))


def _pad_heads(w, heads, width, padded):
    lead = w.shape[:-1]
    w = w.reshape(lead + (heads, width))
    w = jnp.pad(w, [(0, 0)] * len(lead) + [(0, 0), (0, padded - width)])
    return w.reshape(lead + (heads * padded,))


def _odd_layouts(w_in, w_gk_fwd, b_gk_fwd, w_gk_bwd, b_gk_bwd, g_norm, w_out):
    kw = GLA_HEADS * GLA_DK
    vw = GLA_HEADS * GLA_DV
    q, k, v, g, z, qm = jnp.split(w_in, [kw, 2 * kw, 2 * kw + vw, 2 * kw + 2 * vw, 2 * kw + 2 * vw + 2 * GLA_RANK], axis=-1)
    z = jnp.pad(z, ((0, 0), (0, GLA_DV_PAD - 2 * GLA_RANK)))
    w_in_pad = jnp.concatenate([
        _pad_heads(q, GLA_HEADS, GLA_DK, GLA_DK_PAD), _pad_heads(k, GLA_HEADS, GLA_DK, GLA_DK_PAD),
        _pad_heads(v, GLA_HEADS, GLA_DV, GLA_DV_PAD), _pad_heads(g, GLA_HEADS, GLA_DV, GLA_DV_PAD), qm, z], axis=-1)

    def per_head(a):
        return jnp.moveaxis(_pad_heads(a, GLA_HEADS, GLA_DK, GLA_DK_PAD).reshape(a.shape[0], GLA_HEADS, GLA_DK_PAD), 1, 0)

    wz = jnp.zeros((GLA_HEADS, GLA_DV_PAD, 2 * GLA_DK_PAD), F32)
    wz = wz.at[:, :GLA_RANK, :GLA_DK_PAD].set(per_head(w_gk_fwd))
    wz = wz.at[:, GLA_RANK:2 * GLA_RANK, GLA_DK_PAD:].set(per_head(w_gk_bwd))
    bz = jnp.concatenate([per_head(b_gk_fwd[None]), per_head(b_gk_bwd[None])], axis=-1)
    gn = jnp.pad(g_norm, (0, GLA_DV_PAD - GLA_DV)).reshape(1, GLA_DV_PAD)
    w_mix = _pad_heads(w_out[:vw].T, GLA_HEADS, GLA_DV, GLA_DV_PAD).T
    return w_in_pad.astype(BF16), wz.astype(BF16), bz.astype(F32), gn.astype(F32), w_mix.astype(BF16), w_out[vw:].astype(BF16)


def _router_split(w_router):
    w = jnp.pad(w_router.astype(F32), ((0, 0), (0, LANES - N_EXPERTS)))
    hi = w.astype(BF16)
    return hi, (w - hi.astype(F32)).astype(BF16)


def kernel(x, mem, even_norm1, even_w_in, even_rpb, even_w_out, even_norm2, even_w_gate, even_w_up, even_w_down, odd_norm1, odd_w_in, odd_w_gk_fwd, odd_b_gk_fwd, odd_w_gk_bwd, odd_b_gk_bwd, odd_g_norm, odd_w_out, odd_norm2, odd_w_router, odd_w_gate, odd_w_up, odd_w_down, mem_norm, w_mem_kv, final_norm):
    b, t, d = x.shape
    m = mem.shape[1]
    n = b * t
    assert even_norm1.shape[0] == 1 and odd_norm1.shape[0] == 1

    mem_kv = _norm_matmul(mem.reshape(b * m, d), mem_norm, w_mem_kv.astype(BF16)).reshape(b, m, 2 * MEM_WIDTH)

    proj = _norm_matmul(x.reshape(n, d), even_norm1[0], even_w_in[0].astype(BF16)).reshape(b, t, -1)
    h = _na_mixer(proj, mem_kv, _na_bias_table(even_rpb[0]), even_w_out[0].astype(BF16), x, rows_per_step=8)
    h = _ffn(h.reshape(n, d), even_norm2[0], even_w_gate[0].astype(BF16), even_w_up[0].astype(BF16),
             even_w_down[0].astype(BF16)).reshape(b, t, d)

    w_in, wz, bz, gn, w_mix, w_mo = _odd_layouts(odd_w_in[0], odd_w_gk_fwd[0], odd_b_gk_fwd[0], odd_w_gk_bwd[0],
                                                odd_b_gk_bwd[0], odd_g_norm[0], odd_w_out[0])
    proj = _norm_matmul(h.reshape(n, d), odd_norm1[0], w_in).reshape(b, t, -1)
    mix = _gla(proj, wz, bz, gn)
    wr_hi, wr_lo = _router_split(odd_w_router[0])
    h, xn_tiles, route = _mix_out_router(mix, proj, mem_kv, w_mix, w_mo, h, odd_norm2[0], wr_hi, wr_lo)
    tok_sorted, dst_sorted, w_sorted, block_expert, block_valid = _route_plan(route.reshape(n, LANES))
    y_tiles = _moe(xn_tiles, block_expert, block_valid, tok_sorted, dst_sorted, w_sorted,
                   odd_w_gate[0].astype(BF16), odd_w_up[0].astype(BF16), odd_w_down[0].astype(BF16))
    return _final(h.reshape(n, d), y_tiles, final_norm).reshape(b, t, d)
```
